```python
import math
import jax, jax.numpy as jnp
from jax import lax
import numpy as np

D_MODEL = 1024
BATCH = 8
SEQ = 2048
DEPTH = 2
DEC_BATCH = 32
DEC_SEQ = 8
PAST_LEN = 16384
PAGE_SIZE = 128

SSM_INNER = D_MODEL
SSM_GROUP = 16
SSM_GROUPS = SSM_INNER // SSM_GROUP
SSM_STATE = 64
DT_MIN = 1e-3
DT_MAX = 1e-1
N_HEADS = 16
HEAD_DIM = D_MODEL // N_HEADS
Q_BLOCK = 128
D_FF = 7 * D_MODEL // 2
N_EXPERTS = 8
TOP_K = 2
D_FF_EXPERT = 7 * D_MODEL // 2
N_SSM_LAYERS = (DEPTH + 1) // 2
N_FOX_LAYERS = DEPTH // 2
RMS_EPS = 1e-6

kernel_name = "hybrid_s5_fox_moe_decode_step"


def rmsnorm(x, g):
    xf = x.astype(jnp.float32)
    y = xf * lax.rsqrt(jnp.mean(xf * xf, axis=-1, keepdims=True) + RMS_EPS) * g.astype(jnp.float32)
    return y.astype(x.dtype)


def _complex_affine_combine(e1, e2):
    ar1, ai1, br1, bi1 = e1
    ar2, ai2, br2, bi2 = e2
    ar = ar2 * ar1 - ai2 * ai1
    ai = ar2 * ai1 + ai2 * ar1
    br = ar2 * br1 - ai2 * bi1 + br2
    bi = ar2 * bi1 + ai2 * br1 + bi2
    return (ar, ai, br, bi)


def s5_mixer(xn, h0_re, h0_im, w_in, lam_re, lam_im, log_dt, b_re, b_im, c_re, c_im, d_skip, w_glu, b_glu, w_out):
    f32 = jnp.float32
    nb, nl = xn.shape[0], xn.shape[1]
    u = (xn @ w_in).astype(f32)
    ug = u.reshape(nb, nl, SSM_GROUPS, SSM_GROUP)
    lr = lam_re.astype(f32)
    li = lam_im.astype(f32)
    dt = jnp.exp(log_dt.astype(f32))[:, None]
    mag = jnp.exp(lr * dt)
    abar_r = mag * jnp.cos(li * dt)
    abar_i = mag * jnp.sin(li * dt)
    den = lr * lr + li * li
    nr = abar_r - 1.0
    coef_r = (nr * lr + abar_i * li) / den
    coef_i = (abar_i * lr - nr * li) / den
    br = b_re.astype(f32)
    bi = b_im.astype(f32)
    bbar_r = coef_r[..., None] * br - coef_i[..., None] * bi
    bbar_i = coef_r[..., None] * bi + coef_i[..., None] * br
    bu_r = jnp.einsum('blgc,gpc->blgp', ug, bbar_r)
    bu_i = jnp.einsum('blgc,gpc->blgp', ug, bbar_i)
    h0r = h0_re.astype(f32)
    h0i = h0_im.astype(f32)
    bu_r = bu_r.at[:, 0].add(abar_r * h0r - abar_i * h0i)
    bu_i = bu_i.at[:, 0].add(abar_r * h0i + abar_i * h0r)
    a_r = jnp.broadcast_to(abar_r, bu_r.shape)
    a_i = jnp.broadcast_to(abar_i, bu_i.shape)
    _, _, h_r, h_i = lax.associative_scan(_complex_affine_combine, (a_r, a_i, bu_r, bu_i), axis=1)
    y = (jnp.einsum('blgp,gcp->blgc', h_r, c_re.astype(f32))
         - jnp.einsum('blgp,gcp->blgc', h_i, c_im.astype(f32)))
    y = y.reshape(nb, nl, SSM_INNER) + d_skip.astype(f32) * u
    g = jax.nn.gelu(y)
    z = g * jax.nn.sigmoid(g @ w_glu.astype(f32) + b_glu.astype(f32))
    out = z.astype(xn.dtype) @ w_out
    return out, h_r[:, -1], h_i[:, -1]


def fox_project(xn, w_in, b_f):
    proj = xn @ w_in
    nb, nl = xn.shape[0], xn.shape[1]
    q = proj[..., :D_MODEL].reshape(nb, nl, N_HEADS, HEAD_DIM)
    k = proj[..., D_MODEL:2 * D_MODEL].reshape(nb, nl, N_HEADS, HEAD_DIM)
    v = proj[..., 2 * D_MODEL:3 * D_MODEL].reshape(nb, nl, N_HEADS, HEAD_DIM)
    logf = jax.nn.log_sigmoid((proj[..., 3 * D_MODEL:] + b_f).astype(jnp.float32))
    return q, k, v, logf


def fox_prompt(xn, w_in, b_f, w_out):
    f32 = jnp.float32
    nb, nl = xn.shape[0], xn.shape[1]
    scale = HEAD_DIM ** -0.5
    q, k, v, logf = fox_project(xn, w_in, b_f)
    c = jnp.cumsum(logf, axis=1).transpose(0, 2, 1)
    outs = []
    for blk in range(nl // Q_BLOCK):
        q0 = blk * Q_BLOCK
        q1 = q0 + Q_BLOCK
        s = jnp.einsum('bqhd,bkhd->bhqk', q[:, q0:q1], k[:, :q1]).astype(f32) * scale
        s = s + c[:, :, q0:q1, None] - c[:, :, None, :q1]
        mask = jnp.arange(q0, q1)[:, None] >= jnp.arange(q1)[None, :]
        p = jax.nn.softmax(jnp.where(mask, s, -jnp.inf), axis=-1)
        outs.append(jnp.einsum('bhqk,bkhd->bqhd', p.astype(v.dtype), v[:, :q1]))
    o = jnp.concatenate(outs, axis=1).reshape(nb, nl, D_MODEL)
    return o @ w_out, k, v, logf


def fox_sample(xn, cache_k, cache_v, cache_logf, layer_idx, page_table, w_in, b_f, w_out):
    f32 = jnp.float32
    nb, ns = xn.shape[0], xn.shape[1]
    n_pages = page_table.shape[1]
    page = cache_k.shape[2]
    scale = HEAD_DIM ** -0.5
    q, k, v, logf = fox_project(xn, w_in, b_f)
    past_lf = cache_logf[layer_idx, page_table].astype(f32).reshape(nb, n_pages * page, N_HEADS)
    c_all = jnp.cumsum(jnp.concatenate([past_lf, logf], axis=1), axis=1)
    c_past = c_all[:, :n_pages * page].reshape(nb, n_pages, page, N_HEADS)
    c_new = c_all[:, n_pages * page:].transpose(0, 2, 1)
    qf = q.astype(f32) * scale

    def page_step(carry, xs):
        m, l, acc = carry
        ids, cp = xs
        kp = cache_k[layer_idx, ids].astype(f32)
        vp = cache_v[layer_idx, ids].astype(f32)
        s = jnp.einsum('bqhd,bkhd->bhqk', qf, kp) + c_new[..., None] - cp.transpose(0, 2, 1)[:, :, None, :]
        m_new = jnp.maximum(m, jnp.max(s, axis=-1))
        alpha = jnp.exp(m - m_new)
        p = jnp.exp(s - m_new[..., None])
        l = l * alpha + jnp.sum(p, axis=-1)
        acc = acc * alpha[..., None] + jnp.einsum('bhqk,bkhd->bhqd', p, vp)
        return (m_new, l, acc), None

    init = (jnp.full((nb, N_HEADS, ns), -jnp.inf, f32),
            jnp.zeros((nb, N_HEADS, ns), f32),
            jnp.zeros((nb, N_HEADS, ns, HEAD_DIM), f32))
    (m, l, acc), _ = lax.scan(page_step, init, (page_table.T, c_past.transpose(1, 0, 2, 3)))
    s = jnp.einsum('bqhd,bkhd->bhqk', qf, k.astype(f32)) + c_new[..., None] - c_new[:, :, None, :]
    mask = jnp.arange(ns)[:, None] >= jnp.arange(ns)[None, :]
    s = jnp.where(mask, s, -jnp.inf)
    m_new = jnp.maximum(m, jnp.max(s, axis=-1))
    alpha = jnp.exp(m - m_new)
    p = jnp.exp(s - m_new[..., None])
    l = l * alpha + jnp.sum(p, axis=-1)
    acc = acc * alpha[..., None] + jnp.einsum('bhqk,bkhd->bhqd', p, v.astype(f32))
    o = (acc / l[..., None]).transpose(0, 2, 1, 3).reshape(nb, ns, D_MODEL).astype(xn.dtype)
    return o @ w_out, k, v, logf


def swiglu(x, w_gate, w_up, w_down):
    return (jax.nn.silu(x @ w_gate) * (x @ w_up)) @ w_down


def moe_swiglu(xn, w_router, b_router, w_gate, w_up, w_down):
    f32 = jnp.float32
    logits = (xn @ w_router).astype(f32) + b_router.astype(f32)
    probs = jax.nn.softmax(logits, axis=-1)
    top_p, top_i = lax.top_k(probs, TOP_K)
    top_p = top_p / jnp.sum(top_p, axis=-1, keepdims=True)
    gates = jnp.sum(jax.nn.one_hot(top_i, N_EXPERTS, dtype=f32) * top_p[..., None], axis=-2)
    out = jnp.zeros(xn.shape, f32)
    for e in range(N_EXPERTS):
        out = out + gates[..., e:e + 1] * swiglu(xn, w_gate[e], w_up[e], w_down[e]).astype(f32)
    return out.astype(xn.dtype)


def setup_inputs(seed: int = 0) -> dict:
    key = jax.random.key(seed)
    ks = iter(jax.random.split(key, 48))
    f32 = jnp.float32

    def nrm(shape, scale):
        return jax.random.normal(next(ks), shape, f32) * scale

    n_pages = PAST_LEN // PAGE_SIZE
    n_used = DEC_BATCH * n_pages
    n_pool = n_used + (n_used + 3) // 4
    page_table = jax.random.permutation(next(ks), n_pool)[:n_used].reshape(DEC_BATCH, n_pages).astype(jnp.int32)

    x_prompt = nrm((BATCH, SEQ, D_MODEL), 1.0)
    x_sample = nrm((DEC_BATCH, DEC_SEQ, D_MODEL), 1.0)
    state_ssm_re = nrm((N_SSM_LAYERS, DEC_BATCH, SSM_GROUPS, SSM_STATE), 0.5)
    state_ssm_im = nrm((N_SSM_LAYERS, DEC_BATCH, SSM_GROUPS, SSM_STATE), 0.5)
    cache_k = nrm((N_FOX_LAYERS, n_pool, PAGE_SIZE, N_HEADS, HEAD_DIM), 1.0)
    cache_v = nrm((N_FOX_LAYERS, n_pool, PAGE_SIZE, N_HEADS, HEAD_DIM), 1.0)
    cache_logf = jax.nn.log_sigmoid(3.0 + nrm((N_FOX_LAYERS, n_pool, PAGE_SIZE, N_HEADS), 1.5))

    norm_mix_pre = 1.0 + nrm((DEPTH, D_MODEL), 0.05)
    norm_mix_post = 1.0 + nrm((DEPTH, D_MODEL), 0.05)
    norm_ffn_pre = 1.0 + nrm((DEPTH, D_MODEL), 0.05)
    norm_ffn_post = 1.0 + nrm((DEPTH, D_MODEL), 0.05)

    ssm_w_in = nrm((N_SSM_LAYERS, D_MODEL, SSM_INNER), D_MODEL ** -0.5)
    ssm_lambda_re = -0.5 + nrm((N_SSM_LAYERS, SSM_GROUPS, SSM_STATE), 0.01)
    ssm_lambda_im = (jnp.pi * jnp.arange(SSM_STATE, dtype=f32))[None, None, :] + nrm((N_SSM_LAYERS, SSM_GROUPS, SSM_STATE), 0.01)
    ssm_log_dt = jax.random.uniform(next(ks), (N_SSM_LAYERS, SSM_GROUPS), f32, math.log(DT_MIN), math.log(DT_MAX))
    ssm_b_re = nrm((N_SSM_LAYERS, SSM_GROUPS, SSM_STATE, SSM_GROUP), (2.0 * SSM_GROUP) ** -0.5)
    ssm_b_im = nrm((N_SSM_LAYERS, SSM_GROUPS, SSM_STATE, SSM_GROUP), (2.0 * SSM_GROUP) ** -0.5)
    ssm_c_re = nrm((N_SSM_LAYERS, SSM_GROUPS, SSM_GROUP, SSM_STATE), (2.0 * SSM_STATE) ** -0.5)
    ssm_c_im = nrm((N_SSM_LAYERS, SSM_GROUPS, SSM_GROUP, SSM_STATE), (2.0 * SSM_STATE) ** -0.5)
    ssm_d = nrm((N_SSM_LAYERS, SSM_INNER), 1.0)
    ssm_w_glu = nrm((N_SSM_LAYERS, SSM_INNER, SSM_INNER), SSM_INNER ** -0.5)
    ssm_b_glu = nrm((N_SSM_LAYERS, SSM_INNER), 0.01)
    ssm_w_out = nrm((N_SSM_LAYERS, SSM_INNER, D_MODEL), SSM_INNER ** -0.5)

    fox_w_in = nrm((N_FOX_LAYERS, D_MODEL, 3 * D_MODEL + N_HEADS), D_MODEL ** -0.5)
    fox_b_f = jax.random.uniform(next(ks), (N_FOX_LAYERS, N_HEADS), f32, 1.0, 5.0)
    fox_w_out = nrm((N_FOX_LAYERS, D_MODEL, D_MODEL), D_MODEL ** -0.5)

    ffn_w_gate = nrm((N_SSM_LAYERS, D_MODEL, D_FF), D_MODEL ** -0.5)
    ffn_w_up = nrm((N_SSM_LAYERS, D_MODEL, D_FF), D_MODEL ** -0.5)
    ffn_w_down = nrm((N_SSM_LAYERS, D_FF, D_MODEL), D_FF ** -0.5)

    moe_w_router = nrm((N_FOX_LAYERS, D_MODEL, N_EXPERTS), D_MODEL ** -0.5)
    moe_b_router = nrm((N_FOX_LAYERS, N_EXPERTS), 0.01)
    moe_w_gate = nrm((N_FOX_LAYERS, N_EXPERTS, D_MODEL, D_FF_EXPERT), D_MODEL ** -0.5)
    moe_w_up = nrm((N_FOX_LAYERS, N_EXPERTS, D_MODEL, D_FF_EXPERT), D_MODEL ** -0.5)
    moe_w_down = nrm((N_FOX_LAYERS, N_EXPERTS, D_FF_EXPERT, D_MODEL), D_FF_EXPERT ** -0.5)

    return {
        "x_prompt": x_prompt, "x_sample": x_sample,
        "state_ssm_re": state_ssm_re, "state_ssm_im": state_ssm_im,
        "cache_k": cache_k, "cache_v": cache_v, "cache_logf": cache_logf,
        "page_table": page_table,
        "norm_mix_pre": norm_mix_pre, "norm_mix_post": norm_mix_post,
        "norm_ffn_pre": norm_ffn_pre, "norm_ffn_post": norm_ffn_post,
        "ssm_w_in": ssm_w_in, "ssm_lambda_re": ssm_lambda_re, "ssm_lambda_im": ssm_lambda_im,
        "ssm_log_dt": ssm_log_dt, "ssm_b_re": ssm_b_re, "ssm_b_im": ssm_b_im,
        "ssm_c_re": ssm_c_re, "ssm_c_im": ssm_c_im, "ssm_d": ssm_d,
        "ssm_w_glu": ssm_w_glu, "ssm_b_glu": ssm_b_glu, "ssm_w_out": ssm_w_out,
        "fox_w_in": fox_w_in, "fox_b_f": fox_b_f, "fox_w_out": fox_w_out,
        "ffn_w_gate": ffn_w_gate, "ffn_w_up": ffn_w_up, "ffn_w_down": ffn_w_down,
        "moe_w_router": moe_w_router, "moe_b_router": moe_b_router,
        "moe_w_gate": moe_w_gate, "moe_w_up": moe_w_up, "moe_w_down": moe_w_down,
    }


def reference(x_prompt, x_sample, state_ssm_re, state_ssm_im, cache_k, cache_v, cache_logf, page_table,
              norm_mix_pre, norm_mix_post, norm_ffn_pre, norm_ffn_post,
              ssm_w_in, ssm_lambda_re, ssm_lambda_im, ssm_log_dt, ssm_b_re, ssm_b_im, ssm_c_re, ssm_c_im,
              ssm_d, ssm_w_glu, ssm_b_glu, ssm_w_out,
              fox_w_in, fox_b_f, fox_w_out,
              ffn_w_gate, ffn_w_up, ffn_w_down,
              moe_w_router, moe_b_router, moe_w_gate, moe_w_up, moe_w_down):
    xp = x_prompt
    xs = x_sample
    ssm_re_p, ssm_im_p, ssm_re_s, ssm_im_s = [], [], [], []
    k_p, v_p, lf_p, k_s, v_s, lf_s = [], [], [], [], [], []
    for layer in range(DEPTH):
        i = layer // 2
        hp = rmsnorm(xp, norm_mix_pre[layer])
        hs = rmsnorm(xs, norm_mix_pre[layer])
        if layer % 2 == 0:
            ssm_args = (ssm_w_in[i], ssm_lambda_re[i], ssm_lambda_im[i], ssm_log_dt[i], ssm_b_re[i], ssm_b_im[i],
                        ssm_c_re[i], ssm_c_im[i], ssm_d[i], ssm_w_glu[i], ssm_b_glu[i], ssm_w_out[i])
            h0 = jnp.zeros((xp.shape[0], SSM_GROUPS, SSM_STATE), jnp.float32)
            mp, hr_p, hi_p = s5_mixer(hp, h0, h0, *ssm_args)
            ms, hr_s, hi_s = s5_mixer(hs, state_ssm_re[i], state_ssm_im[i], *ssm_args)
            ssm_re_p.append(hr_p); ssm_im_p.append(hi_p)
            ssm_re_s.append(hr_s); ssm_im_s.append(hi_s)
        else:
            mp, kp_, vp_, lfp_ = fox_prompt(hp, fox_w_in[i], fox_b_f[i], fox_w_out[i])
            ms, ks_, vs_, lfs_ = fox_sample(hs, cache_k, cache_v, cache_logf, i, page_table,
                                            fox_w_in[i], fox_b_f[i], fox_w_out[i])
            k_p.append(kp_); v_p.append(vp_); lf_p.append(lfp_)
            k_s.append(ks_); v_s.append(vs_); lf_s.append(lfs_)
        xp = xp + rmsnorm(mp, norm_mix_post[layer])
        xs = xs + rmsnorm(ms, norm_mix_post[layer])
        hp = rmsnorm(xp, norm_ffn_pre[layer])
        hs = rmsnorm(xs, norm_ffn_pre[layer])
        if layer % 2 == 0:
            fp = swiglu(hp, ffn_w_gate[i], ffn_w_up[i], ffn_w_down[i])
            fs = swiglu(hs, ffn_w_gate[i], ffn_w_up[i], ffn_w_down[i])
        else:
            fp = moe_swiglu(hp, moe_w_router[i], moe_b_router[i], moe_w_gate[i], moe_w_up[i], moe_w_down[i])
            fs = moe_swiglu(hs, moe_w_router[i], moe_b_router[i], moe_w_gate[i], moe_w_up[i], moe_w_down[i])
        xp = xp + rmsnorm(fp, norm_ffn_post[layer])
        xs = xs + rmsnorm(fs, norm_ffn_post[layer])
    ssm_re_prompt = jnp.stack(ssm_re_p, axis=0)
    ssm_im_prompt = jnp.stack(ssm_im_p, axis=0)
    ssm_re_sample = jnp.stack(ssm_re_s, axis=0)
    ssm_im_sample = jnp.stack(ssm_im_s, axis=0)
    k_prompt = jnp.stack(k_p, axis=0)
    v_prompt = jnp.stack(v_p, axis=0)
    logf_prompt = jnp.stack(lf_p, axis=0)
    k_sample = jnp.stack(k_s, axis=0)
    v_sample = jnp.stack(v_s, axis=0)
    logf_sample = jnp.stack(lf_s, axis=0)
    return (xp, xs, ssm_re_prompt, ssm_im_prompt, ssm_re_sample, ssm_im_sample,
            k_prompt, v_prompt, logf_prompt, k_sample, v_sample, logf_sample)
```

```python
import functools

import jax
import jax.numpy as jnp
from jax import lax
from jax.experimental import pallas as pl
from jax.experimental.pallas import tpu as pltpu

F32 = jnp.float32
BF16 = jnp.bfloat16
RMS_EPS = 1e-6
LANES = 128
SUBLANES = 8
V7X_VMEM_LIMIT_BYTES = 56 << 20
S5_CHUNK = 8
NEG_INF = float("-inf")


def _cparams(*sem):
    return pltpu.CompilerParams(dimension_semantics=sem, vmem_limit_bytes=V7X_VMEM_LIMIT_BYTES)


def _rms(x, g):
    return x * lax.rsqrt(jnp.mean(x * x, axis=-1, keepdims=True) + RMS_EPS) * g


def _dot(a, b):
    return jnp.dot(a, b, preferred_element_type=F32)


def _dot_nt(a, b):
    return lax.dot_general(a, b, (((1,), (1,)), ((), ())), preferred_element_type=F32)


def _split3(x):
    hi = x.astype(BF16)
    r = x - hi.astype(F32)
    mid = r.astype(BF16)
    lo = (r - mid.astype(F32)).astype(BF16)
    return hi, mid, lo


def _dot_exact_rhs01(x, m01):
    hi, mid, lo = _split3(x)
    return _dot(hi, m01) + _dot(mid, m01) + _dot(lo, m01)


def _dot_exact_lhs01(m01, x):
    hi, mid, lo = _split3(x)
    return _dot(m01, hi) + _dot(m01, mid) + _dot(m01, lo)


def _norm_matmul_kernel(x_ref, g_ref, w_ref, o_ref):
    xn = _rms(x_ref[...], g_ref[...]).astype(BF16)
    o_ref[...] = _dot(xn, w_ref[...])


def _norm_matmul(x, g, w, tm):
    t, d = x.shape
    n = w.shape[1]
    return pl.pallas_call(
        _norm_matmul_kernel,
        grid=(t // tm,),
        in_specs=[pl.BlockSpec((tm, d), lambda i: (i, 0)),
                  pl.BlockSpec((1, d), lambda i: (0, 0)),
                  pl.BlockSpec((d, n), lambda i: (0, 0))],
        out_specs=pl.BlockSpec((tm, n), lambda i: (i, 0)),
        out_shape=jax.ShapeDtypeStruct((t, n), F32),
        compiler_params=_cparams("parallel"),
        name="norm_matmul",
    )(x, g, w)


def _s5_prepare(lam_re, lam_im, log_dt, b_re, b_im, c_re, c_im):
    tc = S5_CHUNK
    g, p = lam_re.shape
    c = b_re.shape[-1]
    ng = (g * c) // LANES
    gl = g // ng
    hp = lax.Precision.HIGHEST
    dt = jnp.exp(log_dt)[:, None]
    mag = jnp.exp(lam_re * dt)
    ar = mag * jnp.cos(lam_im * dt)
    ai = mag * jnp.sin(lam_im * dt)
    den = lam_re * lam_re + lam_im * lam_im
    nr = ar - 1.0
    coef_r = (nr * lam_re + ai * lam_im) / den
    coef_i = (ai * lam_re - nr * lam_im) / den
    bbr = coef_r[..., None] * b_re - coef_i[..., None] * b_im
    bbi = coef_r[..., None] * b_im + coef_i[..., None] * b_re
    prs, pis = [jnp.ones_like(ar)], [jnp.zeros_like(ar)]
    for _ in range(tc):
        nr_, ni_ = prs[-1] * ar - pis[-1] * ai, prs[-1] * ai + pis[-1] * ar
        prs.append(nr_)
        pis.append(ni_)
    pr = jnp.stack(prs)
    pi = jnp.stack(pis)
    eye = jnp.eye(gl, dtype=F32)

    prs_ = pr[tc - 1 - jnp.arange(tc)][..., None]
    pis_ = pi[tc - 1 - jnp.arange(tc)][..., None]
    ms = jnp.stack([prs_ * bbr - pis_ * bbi, prs_ * bbi + pis_ * bbr])
    ms = ms.transpose(2, 1, 4, 0, 3).reshape(ng, gl, tc, c, 2, p)
    w_state = jnp.einsum("agtcrp,gh->atgcrhp", ms, eye).reshape(ng, tc * gl * c, 2 * gl * p)

    er = pr[:tc, :, :, None] * bbr - pi[:tc, :, :, None] * bbi
    ei = pr[:tc, :, :, None] * bbi + pi[:tc, :, :, None] * bbr
    kk = (jnp.einsum("gdp,kgpc->kgdc", c_re, er, precision=hp)
          - jnp.einsum("gdp,kgpc->kgdc", c_im, ei, precision=hp))
    lag = jnp.arange(tc)[None, :] - jnp.arange(tc)[:, None]
    kt = jnp.where((lag >= 0)[:, :, None, None, None], kk[jnp.clip(lag, 0)], 0.0)
    kt = kt.transpose(2, 0, 4, 1, 3).reshape(ng, gl, tc, c, tc, c)
    w_intra = jnp.einsum("agtcsd,gh->atgcshd", kt, eye).reshape(ng, tc * gl * c, tc * gl * c)

    pr1 = pr[1:].transpose(1, 2, 0)[:, :, :, None]
    pi1 = pi[1:].transpose(1, 2, 0)[:, :, :, None]
    cre = c_re.transpose(0, 2, 1)[:, :, None, :]
    cim = c_im.transpose(0, 2, 1)[:, :, None, :]
    hh = jnp.stack([cre * pr1 - cim * pi1, -(cre * pi1 + cim * pr1)])
    hh = hh.reshape(2, ng, gl, p, tc, c)
    w_carry = jnp.einsum("ragpsd,gh->argpshd", hh, eye).reshape(ng, 2 * gl * p, tc * gl * c)

    a_pow = jnp.concatenate([pr[tc].reshape(ng, 1, gl * p), pi[tc].reshape(ng, 1, gl * p)], axis=-1)
    return w_state.astype(BF16), w_intra.astype(BF16), w_carry.astype(BF16), a_pow


def _s5_core_kernel(u_ref, h0_ref, ws_ref, wi_ref, wh_ref, ap_ref, d_ref, g_ref, hout_ref,
                    s_scr, hs_scr, hcar, *, nb, nj):
    tc = S5_CHUNK
    rows = nb * nj
    nk = s_scr.shape[0]
    half = nk // 2
    lane_blk = lambda k: slice(k * LANES, (k + 1) * LANES)

    @pl.when(pl.program_id(1) == 0)
    def _():
        hcar[...] = h0_ref[...]

    def u_at(t):
        if nj == 1:
            return u_ref[:, t, :]
        return u_ref[:, pl.ds(t, nj, stride=tc), :].reshape(rows, LANES)

    us = [u_at(t) for t in range(tc)]
    x = jnp.concatenate([v.astype(BF16) for v in us], axis=1)
    s_all = _dot(x, ws_ref[...])
    for k in range(nk):
        s_scr[k] = s_all[:, lane_blk(k)]

    a_pow = [ap_ref[:, lane_blk(k)] for k in range(nk)]

    def rows_at(j):
        return slice(None) if nj == 1 else pl.ds(j, nb, stride=nj)

    def step(j, h):
        for k in range(nk):
            hs_scr[k, rows_at(j), :] = h[k]
        new_re, new_im = [], []
        for k in range(half):
            hr, hi, ar, ai = h[k], h[k + half], a_pow[k], a_pow[k + half]
            new_re.append(ar * hr - ai * hi + s_scr[k, rows_at(j), :])
            new_im.append(ar * hi + ai * hr + s_scr[k + half, rows_at(j), :])
        return tuple(new_re + new_im)

    h = tuple(hcar[:, lane_blk(k)] for k in range(nk))
    h = step(0, h) if nj == 1 else lax.fori_loop(0, nj, step, h)
    for k in range(nk):
        hcar[:, lane_blk(k)] = h[k]
        hout_ref[:, lane_blk(k)] = h[k]

    hs = jnp.concatenate([hs_scr[k].astype(BF16) for k in range(nk)], axis=1)
    y = _dot(x, wi_ref[...]) + _dot(hs, wh_ref[...])
    d = d_ref[...]
    for t in range(tc):
        gt = jax.nn.gelu(y[:, lane_blk(t)] + d * us[t])
        if nj == 1:
            g_ref[:, t, :] = gt
        else:
            g_ref[:, pl.ds(t, nj, stride=tc), :] = gt.reshape(nb, nj, LANES)


def _s5_core(u, h0, w_state, w_intra, w_carry, a_pow, d_skip):
    nb, nl, d = u.shape
    ng = d // LANES
    ts = min(nl, 512)
    nj = ts // S5_CHUNK
    sw = w_state.shape[2]
    kern = functools.partial(_s5_core_kernel, nb=nb, nj=nj)
    return pl.pallas_call(
        kern,
        grid=(ng, nl // ts),
        in_specs=[pl.BlockSpec((nb, ts, LANES), lambda g, s: (0, s, g)),
                  pl.BlockSpec((None, nb, sw), lambda g, s: (g, 0, 0)),
                  pl.BlockSpec((None,) + w_state.shape[1:], lambda g, s: (g, 0, 0)),
                  pl.BlockSpec((None,) + w_intra.shape[1:], lambda g, s: (g, 0, 0)),
                  pl.BlockSpec((None,) + w_carry.shape[1:], lambda g, s: (g, 0, 0)),
                  pl.BlockSpec((None, 1, sw), lambda g, s: (g, 0, 0)),
                  pl.BlockSpec((1, LANES), lambda g, s: (0, g))],
        out_specs=[pl.BlockSpec((nb, ts, LANES), lambda g, s: (0, s, g)),
                   pl.BlockSpec((None, nb, sw), lambda g, s: (g, 0, 0))],
        out_shape=[jax.ShapeDtypeStruct((nb, nl, d), F32),
                   jax.ShapeDtypeStruct((ng, nb, sw), F32)],
        scratch_shapes=[pltpu.VMEM((sw // LANES, nb * nj, LANES), F32),
                        pltpu.VMEM((sw // LANES, nb * nj, LANES), F32),
                        pltpu.VMEM((nb, sw), F32)],
        compiler_params=_cparams("parallel", "arbitrary"),
        name="s5_core",
    )(u, h0, w_state, w_intra, w_carry, a_pow, d_skip)


def _s5_state_in(re, im, ng):
    nb = re.shape[0]
    h = jnp.concatenate([re.reshape(nb, ng, -1), im.reshape(nb, ng, -1)], axis=-1)
    return h.transpose(1, 0, 2)


def _s5_state_out(h, g, p):
    ng, nb, sw = h.shape
    h = h.transpose(1, 0, 2)
    return h[..., :sw // 2].reshape(nb, g, p), h[..., sw // 2:].reshape(nb, g, p)


def _glu_out_kernel(g_ref, x_ref, wglu_ref, bglu_ref, wout_ref, gpost_ref, o_ref):
    g = g_ref[...]
    t = _dot(g.astype(BF16), wglu_ref[...]) + bglu_ref[...]
    z = (g * jax.nn.sigmoid(t)).astype(BF16)
    m = _dot(z, wout_ref[...])
    o_ref[...] = x_ref[...] + _rms(m, gpost_ref[...])


def _glu_out(g, x, w_glu, b_glu, w_out, g_post, tm):
    t, d = x.shape
    row = pl.BlockSpec((tm, d), lambda i: (i, 0))
    vec = pl.BlockSpec((1, d), lambda i: (0, 0))
    mat = pl.BlockSpec((d, d), lambda i: (0, 0))
    return pl.pallas_call(
        _glu_out_kernel,
        grid=(t // tm,),
        in_specs=[row, row, mat, vec, mat, vec],
        out_specs=row,
        out_shape=jax.ShapeDtypeStruct((t, d), F32),
        compiler_params=_cparams("parallel"),
        name="glu_out",
    )(g, x, w_glu, b_glu, w_out, g_post)


def _matmul_post_kernel(a_ref, x_ref, w_ref, gpost_ref, o_ref):
    m = _dot(a_ref[...].astype(BF16), w_ref[...])
    o_ref[...] = x_ref[...] + _rms(m, gpost_ref[...])


def _matmul_post(a, x, w, g_post, tm):
    t, d = x.shape
    row = pl.BlockSpec((tm, d), lambda i: (i, 0))
    return pl.pallas_call(
        _matmul_post_kernel,
        grid=(t // tm,),
        in_specs=[row, row, pl.BlockSpec((d, d), lambda i: (0, 0)), pl.BlockSpec((1, d), lambda i: (0, 0))],
        out_specs=row,
        out_shape=jax.ShapeDtypeStruct((t, d), F32),
        compiler_params=_cparams("parallel"),
        name="matmul_post",
    )(a, x, w, g_post)


def _ffn_kernel(*refs, gated):
    if gated:
        x_ref, gpre_ref, wg_ref, wu_ref, wd_ref, gpost_ref, gates_ref, o_ref, xn_scr, acc_scr = refs
    else:
        x_ref, gpre_ref, wg_ref, wu_ref, wd_ref, gpost_ref, o_ref, xn_scr, acc_scr = refs
    e, f = pl.program_id(1), pl.program_id(2)
    ne, nf = pl.num_programs(1), pl.num_programs(2)

    @pl.when((e == 0) & (f == 0))
    def _():
        xn_scr[...] = _rms(x_ref[...], gpre_ref[...]).astype(BF16)
        acc_scr[...] = jnp.zeros_like(acc_scr)

    xn = xn_scr[...]
    a = _dot(xn, wg_ref[...])
    b = _dot(xn, wu_ref[...])
    h = (jax.nn.silu(a) * b).astype(BF16)
    y = _dot(h, wd_ref[...])
    if gated:
        lane = lax.broadcasted_iota(jnp.int32, gates_ref.shape, 1)
        gate = jnp.sum(jnp.where(lane == e, gates_ref[...], 0.0), axis=1, keepdims=True)
        y = gate * y
    acc_scr[...] += y

    @pl.when((e == ne - 1) & (f == nf - 1))
    def _():
        o_ref[...] = x_ref[...] + _rms(acc_scr[...], gpost_ref[...])


def _ffn(x, g_pre, w_gate, w_up, w_down, g_post, gates, tm, tf):
    t, d = x.shape
    ne, _, ff = w_gate.shape
    gated = gates is not None
    row = pl.BlockSpec((tm, d), lambda i, e, f: (i, 0))
    vec = pl.BlockSpec((1, d), lambda i, e, f: (0, 0))
    in_specs = [row, vec,
                pl.BlockSpec((None, d, tf), lambda i, e, f: (e, 0, f)),
                pl.BlockSpec((None, d, tf), lambda i, e, f: (e, 0, f)),
                pl.BlockSpec((None, tf, d), lambda i, e, f: (e, f, 0)),
                vec]
    args = [x, g_pre, w_gate, w_up, w_down, g_post]
    if gated:
        in_specs.append(pl.BlockSpec((tm, LANES), lambda i, e, f: (i, 0)))
        args.append(gates)
    return pl.pallas_call(
        functools.partial(_ffn_kernel, gated=gated),
        grid=(t // tm, ne, ff // tf),
        in_specs=in_specs,
        out_specs=row,
        out_shape=jax.ShapeDtypeStruct((t, d), F32),
        scratch_shapes=[pltpu.VMEM((tm, d), BF16), pltpu.VMEM((tm, d), F32)],
        compiler_params=_cparams("parallel", "arbitrary", "arbitrary"),
        name="ffn_gated" if gated else "ffn_dense",
    )(*args)


def _router_kernel(x_ref, gpre_ref, wr_ref, br_ref, gates_ref):
    xn = _rms(x_ref[...], gpre_ref[...]).astype(BF16)
    logits = _dot(xn, wr_ref[...]) + br_ref[...]
    ex = jnp.exp(logits - jnp.max(logits, axis=-1, keepdims=True))
    probs = ex / jnp.sum(ex, axis=-1, keepdims=True)
    lane = lax.broadcasted_iota(jnp.int32, probs.shape, 1)
    p1 = jnp.max(probs, axis=-1, keepdims=True)
    i1 = jnp.min(jnp.where(probs == p1, lane, LANES), axis=-1, keepdims=True)
    rest = jnp.where(lane == i1, -1.0, probs)
    p2 = jnp.max(rest, axis=-1, keepdims=True)
    i2 = jnp.min(jnp.where(rest == p2, lane, LANES), axis=-1, keepdims=True)
    den = p1 + p2
    gates_ref[...] = jnp.where(lane == i1, p1 / den, 0.0) + jnp.where(lane == i2, p2 / den, 0.0)


def _router(x, g_pre, w_router, b_router, tm):
    t, d = x.shape
    ne = w_router.shape[1]
    wr = jnp.zeros((d, LANES), BF16).at[:, :ne].set(w_router.astype(BF16))
    br = jnp.full((1, LANES), -1e30, F32).at[0, :ne].set(b_router)
    return pl.pallas_call(
        _router_kernel,
        grid=(t // tm,),
        in_specs=[pl.BlockSpec((tm, d), lambda i: (i, 0)),
                  pl.BlockSpec((1, d), lambda i: (0, 0)),
                  pl.BlockSpec((d, LANES), lambda i: (0, 0)),
                  pl.BlockSpec((1, LANES), lambda i: (0, 0))],
        out_specs=pl.BlockSpec((tm, LANES), lambda i: (i, 0)),
        out_shape=jax.ShapeDtypeStruct((t, LANES), F32),
        compiler_params=_cparams("parallel"),
        name="moe_router",
    )(x, g_pre, wr, br)


def _log_sigmoid(z):
    return jnp.minimum(z, 0.0) - jnp.log(1.0 + jnp.exp(-jnp.abs(z)))


def _fox_proj_prompt_kernel(x_ref, gpre_ref, wq_ref, wkvt_ref, wft_ref, bf_ref,
                            q_ref, kt_ref, vt_ref, lft_ref, *, scale):
    xn = _rms(x_ref[...], gpre_ref[...]).astype(BF16)
    q_ref[...] = (_dot(xn, wq_ref[...]) * scale).astype(BF16)
    d = kt_ref.shape[0]
    kvt = _dot_nt(wkvt_ref[...], xn)
    kt_ref[...] = kvt[:d]
    vt_ref[...] = kvt[d:]
    lft_ref[...] = _log_sigmoid(_dot_nt(wft_ref[...], xn) + bf_ref[...])


def _fox_proj_prompt(x, g_pre, wq, wkvt, wft, b_f, scale, tm):
    nb, nl, d = x.shape
    nh = wft.shape[0]
    const = lambda b, i: (0, 0)
    return pl.pallas_call(
        functools.partial(_fox_proj_prompt_kernel, scale=scale),
        grid=(nb, nl // tm),
        in_specs=[pl.BlockSpec((None, tm, d), lambda b, i: (b, i, 0)),
                  pl.BlockSpec((1, d), const),
                  pl.BlockSpec(wq.shape, const),
                  pl.BlockSpec(wkvt.shape, const),
                  pl.BlockSpec(wft.shape, const),
                  pl.BlockSpec((nh, 1), const)],
        out_specs=[pl.BlockSpec((None, tm, d), lambda b, i: (b, i, 0)),
                   pl.BlockSpec((None, d, tm), lambda b, i: (b, 0, i)),
                   pl.BlockSpec((None, d, tm), lambda b, i: (b, 0, i)),
                   pl.BlockSpec((None, nh, tm), lambda b, i: (b, 0, i))],
        out_shape=[jax.ShapeDtypeStruct((nb, nl, d), BF16),
                   jax.ShapeDtypeStruct((nb, d, nl), F32),
                   jax.ShapeDtypeStruct((nb, d, nl), F32),
                   jax.ShapeDtypeStruct((nb, nh, nl), F32)],
        compiler_params=_cparams("parallel", "parallel"),
        name="fox_proj_prompt",
    )(x, g_pre, wq, wkvt, wft, b_f)


def _tri01(n, rel):
    r = lax.broadcasted_iota(jnp.int32, (n, n), 0)
    c = lax.broadcasted_iota(jnp.int32, (n, n), 1)
    return rel(r, c).astype(BF16)


def _cumsum_lanes_kernel(x_ref, o_ref):
    upper = _tri01(LANES, lambda r, c: r <= c)
    carry = jnp.zeros((x_ref.shape[0], 1), F32)
    for blk in range(x_ref.shape[1] // LANES):
        sl = slice(blk * LANES, (blk + 1) * LANES)
        cum = _dot_exact_rhs01(x_ref[:, sl], upper) + carry
        o_ref[:, sl] = cum
        carry = cum[:, LANES - 1:LANES]


def _cumsum_lanes(x):
    nb, r, nl = x.shape
    spec = pl.BlockSpec((None, r, nl), lambda b: (b, 0, 0))
    return pl.pallas_call(
        _cumsum_lanes_kernel, grid=(nb,), in_specs=[spec], out_specs=spec,
        out_shape=jax.ShapeDtypeStruct(x.shape, F32),
        compiler_params=_cparams("parallel"), name="logf_cumsum",
    )(x)


def _fox_attn_prompt_kernel(q_ref, kt_ref, vt_ref, c_ref, o_ref, *, tq, hd):
    hpair, qi = pl.program_id(1), pl.program_id(2)
    q0 = pl.multiple_of(qi * tq, tq)
    row = lax.broadcasted_iota(jnp.int32, (tq, tq), 0)
    col = lax.broadcasted_iota(jnp.int32, (tq, tq), 1)
    for hh in range(LANES // hd):
        hs = slice(hh * hd, (hh + 1) * hd)
        q = q_ref[pl.ds(q0, tq), hs]

        def tile(k0, carry, masked, hs=hs, hh=hh, q=q):
            m, l, acc = carry
            kt = kt_ref[hs, pl.ds(k0, tq)].astype(BF16)
            vt = vt_ref[hs, pl.ds(k0, tq)].astype(BF16)
            cc = c_ref[pl.ds(hpair * (LANES // hd) + hh, 1), pl.ds(k0, tq)]
            s = _dot(q, kt) - cc
            if masked:
                s = jnp.where(row >= col, s, NEG_INF)
            m_new = jnp.maximum(m, jnp.max(s, axis=-1, keepdims=True))
            alpha = jnp.exp(m - m_new)
            p = jnp.exp(s - m_new)
            l = alpha * l + jnp.sum(p, axis=-1, keepdims=True)
            acc = alpha * acc + _dot_nt(p.astype(BF16), vt)
            return m_new, l, acc

        init = (jnp.full((tq, 1), NEG_INF, F32), jnp.zeros((tq, 1), F32), jnp.zeros((tq, hd), F32))
        carry = lax.fori_loop(0, qi, lambda j, cr: tile(pl.multiple_of(j * tq, tq), cr, False), init)
        _, l, acc = tile(q0, carry, True)
        o_ref[:, hs] = (acc / l).astype(o_ref.dtype)


def _fox_attn_prompt(q, kt, vt, c, hd, tq):
    nb, nl, d = q.shape
    nh = c.shape[1]
    return pl.pallas_call(
        functools.partial(_fox_attn_prompt_kernel, tq=tq, hd=hd),
        grid=(nb, d // LANES, nl // tq),
        in_specs=[pl.BlockSpec((None, nl, LANES), lambda b, h, i: (b, 0, h)),
                  pl.BlockSpec((None, LANES, nl), lambda b, h, i: (b, h, 0)),
                  pl.BlockSpec((None, LANES, nl), lambda b, h, i: (b, h, 0)),
                  pl.BlockSpec((None, nh, nl), lambda b, h, i: (b, 0, 0))],
        out_specs=pl.BlockSpec((None, tq, LANES), lambda b, h, i: (b, i, h)),
        out_shape=jax.ShapeDtypeStruct((nb, nl, d), BF16),
        compiler_params=_cparams("parallel", "parallel", "arbitrary"),
        name="fox_attn_prompt",
    )(q, kt, vt, c)


def _fox_proj_sample_kernel(x_ref, gpre_ref, w_ref, wf_ref, bf_ref, q_ref, k_ref, v_ref, lf_ref, *, scale):
    xn = _rms(x_ref[...], gpre_ref[...]).astype(BF16)
    d = q_ref.shape[1]
    proj = _dot(xn, w_ref[...])
    q_ref[...] = proj[:, :d] * scale
    k_ref[...] = proj[:, d:2 * d]
    v_ref[...] = proj[:, 2 * d:]
    lf_ref[...] = _log_sigmoid(_dot(xn, wf_ref[...]) + bf_ref[...])


def _fox_proj_sample(x, g_pre, w_qkv, w_f, b_f, scale):
    t, d = x.shape
    nh = w_f.shape[1]
    wf = jnp.zeros((d, LANES), BF16).at[:, :nh].set(w_f)
    bf = jnp.zeros((1, LANES), F32).at[0, :nh].set(b_f)
    full = lambda a: pl.BlockSpec(a.shape, lambda i: (0,) * a.ndim)
    row = pl.BlockSpec((t, d), lambda i: (0, 0))
    return pl.pallas_call(
        functools.partial(_fox_proj_sample_kernel, scale=scale),
        grid=(1,),
        in_specs=[row, full(g_pre), full(w_qkv), full(wf), full(bf)],
        out_specs=[row, row, row, pl.BlockSpec((t, LANES), lambda i: (0, 0))],
        out_shape=[jax.ShapeDtypeStruct((t, d), F32)] * 3 + [jax.ShapeDtypeStruct((t, LANES), F32)],
        compiler_params=_cparams("arbitrary"),
        name="fox_proj_sample",
    )(x, g_pre, w_qkv, wf, bf)


def _paged_attn_kernel(tbl_ref, q_ref, kc_ref, vc_ref, lfc_ref, kn_ref, vn_ref, lfn_ref, o_ref,
                       qbd, m_scr, l_scr, acc_scr, carry, *, ns, hd):
    del tbl_ref
    p, n_pages = pl.program_id(1), pl.num_programs(1)
    rows, d = qbd.shape
    nh = d // hd
    r_i = lax.broadcasted_iota(jnp.int32, (rows, d), 0)
    c_i = lax.broadcasted_iota(jnp.int32, (rows, d), 1)
    bmask = (r_i // ns) == (c_i // hd)
    expand = (lax.broadcasted_iota(jnp.int32, (rows, nh), 0) // ns
              == lax.broadcasted_iota(jnp.int32, (rows, nh), 1)).astype(BF16)

    @pl.when(p == 0)
    def _():
        qt = jnp.concatenate([q_ref[...]] * nh, axis=0)
        qbd[...] = jnp.where(bmask, qt, 0.0).astype(BF16)
        m_scr[...] = jnp.full_like(m_scr, NEG_INF)
        l_scr[...] = jnp.zeros_like(l_scr)
        acc_scr[...] = jnp.zeros_like(acc_scr)
        carry[...] = jnp.zeros_like(carry)

    def update(s, v_nt=None, v_nn=None):
        m_new = jnp.maximum(m_scr[...], jnp.max(s, axis=-1, keepdims=True))
        alpha = jnp.exp(m_scr[...] - m_new)
        pr = jnp.exp(s - m_new)
        l_scr[...] = alpha * l_scr[...] + jnp.sum(pr, axis=-1, keepdims=True)
        pv = _dot_nt(pr.astype(BF16), v_nt) if v_nt is not None else _dot(pr.astype(BF16), v_nn)
        acc_scr[...] = alpha * acc_scr[...] + pv
        m_scr[...] = m_new

    lf = lfc_ref[...]
    after = _tri01(LANES, lambda r, c: r > c)
    bias_h = carry[...] + _dot_exact_rhs01(lf, after)
    carry[...] = carry[...] + jnp.sum(lf, axis=-1, keepdims=True)
    s = _dot(qbd[...], kc_ref[...].astype(BF16)) + _dot_exact_lhs01(expand, bias_h)
    update(s, v_nt=vc_ref[...].astype(BF16))

    @pl.when(p == n_pages - 1)
    def _():
        upto = _tri01(LANES, lambda r, c: r <= c)
        cn = _dot_exact_rhs01(lfn_ref[...], upto)
        pad = jnp.zeros((LANES - ns, d), F32)
        kn = jnp.concatenate([kn_ref[...], pad], axis=0).astype(BF16)
        vn = jnp.concatenate([vn_ref[...], pad], axis=0).astype(BF16)
        sn = _dot_nt(qbd[...], kn) - _dot_exact_lhs01(expand, cn)
        qpos = lax.broadcasted_iota(jnp.int32, (rows, LANES), 0) % ns
        kpos = lax.broadcasted_iota(jnp.int32, (rows, LANES), 1)
        update(jnp.where(kpos <= qpos, sn, NEG_INF), v_nn=vn)
        o = jnp.where(bmask, acc_scr[...] / l_scr[...], 0.0)
        out = o[0:ns]
        for h in range(1, nh):
            out = out + o[h * ns:(h + 1) * ns]
        o_ref[...] = out


def _paged_attn(page_table, q, kc, vc, lfc, kn, vn, lfn, hd):
    db, ns, d = q.shape
    n_pages = page_table.shape[1]
    nh = d // hd
    rows = nh * ns
    page = kc.shape[2]
    assert rows == LANES and page == LANES
    seq = pl.BlockSpec((None, ns, d), lambda b, p, tbl: (b, 0, 0))
    pg = lambda b, p, tbl: (tbl[b, n_pages - 1 - p], 0, 0)
    grid_spec = pltpu.PrefetchScalarGridSpec(
        num_scalar_prefetch=1,
        grid=(db, n_pages),
        in_specs=[seq,
                  pl.BlockSpec((None, d, page), pg),
                  pl.BlockSpec((None, d, page), pg),
                  pl.BlockSpec((None, nh, page), pg),
                  seq, seq,
                  pl.BlockSpec((None, nh, LANES), lambda b, p, tbl: (b, 0, 0))],
        out_specs=seq,
        scratch_shapes=[pltpu.VMEM((rows, d), BF16),
                        pltpu.VMEM((rows, 1), F32),
                        pltpu.VMEM((rows, 1), F32),
                        pltpu.VMEM((rows, d), F32),
                        pltpu.VMEM((nh, 1), F32)],
    )
    return pl.pallas_call(
        functools.partial(_paged_attn_kernel, ns=ns, hd=hd),
        grid_spec=grid_spec,
        out_shape=jax.ShapeDtypeStruct((db, ns, d), F32),
        compiler_params=_cparams("parallel", "arbitrary"),
        name="fox_paged_attn",
    )(page_table, q, kc, vc, lfc, kn, vn, lfn)


def _row_tile(t):
    for tm in (1024, 512, 256, 128):
        if t % tm == 0:
            return tm
    return t


def kernel(x_prompt, x_sample, state_ssm_re, state_ssm_im, cache_k, cache_v, cache_logf, page_table, norm_mix_pre, norm_mix_post, norm_ffn_pre, norm_ffn_post, ssm_w_in, ssm_lambda_re, ssm_lambda_im, ssm_log_dt, ssm_b_re, ssm_b_im, ssm_c_re, ssm_c_im, ssm_d, ssm_w_glu, ssm_b_glu, ssm_w_out, fox_w_in, fox_b_f, fox_w_out, ffn_w_gate, ffn_w_up, ffn_w_down, moe_w_router, moe_b_router, moe_w_gate, moe_w_up, moe_w_down):
    nb, nl, d = x_prompt.shape
    db, ns, _ = x_sample.shape
    n_groups, n_state = ssm_lambda_re.shape[1:]
    nh = fox_b_f.shape[1]
    hd = d // nh
    scale = hd ** -0.5
    ng = d // LANES
    streams = [x_prompt.reshape(nb * nl, d), x_sample.reshape(db * ns, d)]
    dims = [(nb, nl), (db, ns)]
    tms = [_row_tile(x.shape[0]) for x in streams]
    vec = lambda a: a.reshape(1, -1)

    li = 0
    w_in = ssm_w_in[li].astype(BF16)
    w_glu = ssm_w_glu[li].astype(BF16)
    w_out = ssm_w_out[li].astype(BF16)
    s5w = _s5_prepare(ssm_lambda_re[li], ssm_lambda_im[li], ssm_log_dt[li], ssm_b_re[li], ssm_b_im[li],
                      ssm_c_re[li], ssm_c_im[li])
    wg, wu, wd = (w.astype(BF16) for w in (ffn_w_gate, ffn_w_up, ffn_w_down))
    h0s = [jnp.zeros((ng, nb, 2 * (n_groups // ng) * n_state), F32),
           _s5_state_in(state_ssm_re[li], state_ssm_im[li], ng)]
    ssm_states = []
    for si in range(2):
        x, (b_, l_), tm = streams[si], dims[si], tms[si]
        u = _norm_matmul(x, vec(norm_mix_pre[0]), w_in, tm)
        g_act, h_fin = _s5_core(u.reshape(b_, l_, d), h0s[si], *s5w, vec(ssm_d[li]))
        ssm_states.append(_s5_state_out(h_fin, n_groups, n_state))
        x = _glu_out(g_act.reshape(b_ * l_, d), x, w_glu, vec(ssm_b_glu[li]), w_out, vec(norm_mix_post[0]), tm)
        streams[si] = _ffn(x, vec(norm_ffn_pre[0]), wg, wu, wd, vec(norm_ffn_post[0]), None, tm, 512)

    w_fox = fox_w_in[li]
    w_fox_t = w_fox.T
    wq = w_fox[:, :d].astype(BF16)
    wkvt = w_fox_t[d:3 * d].astype(BF16)
    wft = w_fox_t[3 * d:].astype(BF16)
    w_o = fox_w_out[li].astype(BF16)

    xp = streams[0]
    q, kt, vt, lft = _fox_proj_prompt(xp.reshape(nb, nl, d), vec(norm_mix_pre[1]), wq, wkvt, wft,
                                      fox_b_f[li].reshape(nh, 1), scale, min(nl, 1024))
    c = _cumsum_lanes(lft)
    o = _fox_attn_prompt(q, kt, vt, c, hd, min(nl, 256))
    xp = _matmul_post(o.reshape(nb * nl, d), xp, w_o, vec(norm_mix_post[1]), tms[0])
    k_prompt = kt.reshape(nb, nh, hd, nl).transpose(0, 3, 1, 2)[None]
    v_prompt = vt.reshape(nb, nh, hd, nl).transpose(0, 3, 1, 2)[None]
    logf_prompt = lft.transpose(0, 2, 1)[None]

    xs = streams[1]
    qs, ks, vs, lfs = _fox_proj_sample(xs, vec(norm_mix_pre[1]), w_fox[:, :3 * d].astype(BF16),
                                       w_fox[:, 3 * d:].astype(BF16), fox_b_f[li], scale)
    n_pool, page = cache_k.shape[1], cache_k.shape[2]
    kc = cache_k[li].transpose(0, 2, 3, 1).reshape(n_pool, d, page)
    vc = cache_v[li].transpose(0, 2, 3, 1).reshape(n_pool, d, page)
    lfc = cache_logf[li].transpose(0, 2, 1)
    lfs = lfs[:, :nh]
    lfn = jnp.zeros((db, nh, LANES), F32).at[:, :, :ns].set(lfs.reshape(db, ns, nh).transpose(0, 2, 1))
    os_ = _paged_attn(page_table, qs.reshape(db, ns, d), kc, vc, lfc,
                      ks.reshape(db, ns, d), vs.reshape(db, ns, d), lfn, hd)
    xs = _matmul_post(os_.reshape(db * ns, d), xs, w_o, vec(norm_mix_post[1]), tms[1])
    k_sample = ks.reshape(1, db, ns, nh, hd)
    v_sample = vs.reshape(1, db, ns, nh, hd)
    logf_sample = lfs.reshape(1, db, ns, nh)

    mg, mu, md = (w[li].astype(BF16) for w in (moe_w_gate, moe_w_up, moe_w_down))
    outs = []
    for x, tm in zip((xp, xs), tms):
        gates = _router(x, vec(norm_ffn_pre[1]), moe_w_router[li], moe_b_router[li], tm)
        outs.append(_ffn(x, vec(norm_ffn_pre[1]), mg, mu, md, vec(norm_ffn_post[1]), gates, tm, 512))

    (re_p, im_p), (re_s, im_s) = ssm_states
    return (outs[0].reshape(nb, nl, d), outs[1].reshape(db, ns, d),
            re_p[None], im_p[None], re_s[None], im_s[None],
            k_prompt, v_prompt, logf_prompt, k_sample, v_sample, logf_sample)
```

```python
import functools

import jax
import jax.numpy as jnp
from jax import lax
from jax.experimental import pallas as pl
from jax.experimental.pallas import tpu as pltpu

F32 = jnp.float32
BF16 = jnp.bfloat16
RMS_EPS = 1e-6
LANES = 128
SUBLANES = 8
V7X_VMEM_LIMIT_BYTES = 56 << 20
S5_CHUNK = 8
MOE_ROW_TILE = 512
NEG_INF = float("-inf")


def _cparams(*sem):
    return pltpu.CompilerParams(dimension_semantics=sem, vmem_limit_bytes=V7X_VMEM_LIMIT_BYTES)


def _rms(x, g):
    return x * lax.rsqrt(jnp.mean(x * x, axis=-1, keepdims=True) + RMS_EPS) * g


def _dot(a, b):
    return jnp.dot(a, b, preferred_element_type=F32)


def _dot_nt(a, b):
    return lax.dot_general(a, b, (((1,), (1,)), ((), ())), preferred_element_type=F32)


def _split3(x):
    hi = x.astype(BF16)
    r = x - hi.astype(F32)
    mid = r.astype(BF16)
    lo = (r - mid.astype(F32)).astype(BF16)
    return hi, mid, lo


def _dot_exact_rhs01(x, m01):
    hi, mid, lo = _split3(x)
    return _dot(hi, m01) + _dot(mid, m01) + _dot(lo, m01)


def _dot_exact_lhs01(m01, x):
    hi, mid, lo = _split3(x)
    return _dot(m01, hi) + _dot(m01, mid) + _dot(m01, lo)


def _norm_matmul_kernel(x_ref, g_ref, w_ref, o_ref):
    xn = _rms(x_ref[...], g_ref[...]).astype(BF16)
    o_ref[...] = _dot(xn, w_ref[...])


def _norm_matmul(x, g, w, tm):
    t, d = x.shape
    n = w.shape[1]
    return pl.pallas_call(
        _norm_matmul_kernel,
        grid=(t // tm,),
        in_specs=[pl.BlockSpec((tm, d), lambda i: (i, 0)),
                  pl.BlockSpec((1, d), lambda i: (0, 0)),
                  pl.BlockSpec((d, n), lambda i: (0, 0))],
        out_specs=pl.BlockSpec((tm, n), lambda i: (i, 0)),
        out_shape=jax.ShapeDtypeStruct((t, n), F32),
        compiler_params=_cparams("parallel"),
        name="norm_matmul",
    )(x, g, w)


def _s5_prepare(lam_re, lam_im, log_dt, b_re, b_im, c_re, c_im):
    tc = S5_CHUNK
    g, p = lam_re.shape
    c = b_re.shape[-1]
    ng = (g * c) // LANES
    gl = g // ng
    hp = lax.Precision.HIGHEST
    dt = jnp.exp(log_dt)[:, None]
    mag = jnp.exp(lam_re * dt)
    ar = mag * jnp.cos(lam_im * dt)
    ai = mag * jnp.sin(lam_im * dt)
    den = lam_re * lam_re + lam_im * lam_im
    nr = ar - 1.0
    coef_r = (nr * lam_re + ai * lam_im) / den
    coef_i = (ai * lam_re - nr * lam_im) / den
    bbr = coef_r[..., None] * b_re - coef_i[..., None] * b_im
    bbi = coef_r[..., None] * b_im + coef_i[..., None] * b_re
    prs, pis = [jnp.ones_like(ar)], [jnp.zeros_like(ar)]
    for _ in range(tc):
        nr_, ni_ = prs[-1] * ar - pis[-1] * ai, prs[-1] * ai + pis[-1] * ar
        prs.append(nr_)
        pis.append(ni_)
    pr = jnp.stack(prs)
    pi = jnp.stack(pis)
    prs_ = pr[tc - 1 - jnp.arange(tc)][..., None]
    pis_ = pi[tc - 1 - jnp.arange(tc)][..., None]
    ms = jnp.stack([prs_ * bbr - pis_ * bbi, prs_ * bbi + pis_ * bbr])
    ms = ms.transpose(2, 1, 4, 0, 3).reshape(ng, gl, tc, c, 2 * p)
    m_state = ms.transpose(0, 2, 1, 3, 4).reshape(ng, tc * gl * c, 2 * p)

    er = pr[:tc, :, :, None] * bbr - pi[:tc, :, :, None] * bbi
    ei = pr[:tc, :, :, None] * bbi + pi[:tc, :, :, None] * bbr
    kk = (jnp.einsum("gdp,kgpc->kgdc", c_re, er, precision=hp)
          - jnp.einsum("gdp,kgpc->kgdc", c_im, ei, precision=hp))
    lag = jnp.arange(tc)[None, :] - jnp.arange(tc)[:, None]
    kt = jnp.where((lag >= 0)[:, :, None, None, None], kk[jnp.clip(lag, 0)], 0.0)
    kt = kt.transpose(2, 0, 4, 1, 3).reshape(ng, gl, tc, c, tc * c)
    m_intra = kt.transpose(0, 2, 1, 3, 4).reshape(ng, tc * gl * c, tc * c)

    pr1 = pr[1:].transpose(1, 2, 0)[:, :, :, None]
    pi1 = pi[1:].transpose(1, 2, 0)[:, :, :, None]
    cre = c_re.transpose(0, 2, 1)[:, :, None, :]
    cim = c_im.transpose(0, 2, 1)[:, :, None, :]
    hh = jnp.stack([cre * pr1 - cim * pi1, -(cre * pi1 + cim * pr1)])
    m_carry = hh.reshape(2, ng, gl * p, tc * c).transpose(1, 0, 2, 3).reshape(ng, 2 * gl * p, tc * c)

    w_state = _s5_expand(m_state, gl, row_group=c, col_block=gl * p, col_sub=p)
    w_intra = _s5_expand(m_intra, gl, row_group=c, col_block=gl * c, col_sub=c)
    w_carry = _s5_expand(m_carry, gl, row_group=p, col_block=gl * c, col_sub=c)
    a_pow = jnp.concatenate([pr[tc].reshape(ng, 1, gl * p), pi[tc].reshape(ng, 1, gl * p)], axis=-1)
    return w_state, w_intra, w_carry, a_pow


def _s5_expand_kernel(m_ref, o_ref, *, gl, row_group, col_block, col_sub):
    rows, cols = o_ref.shape
    s_i = lax.broadcasted_iota(jnp.int32, (m_ref.shape[1], cols), 0)
    j_i = lax.broadcasted_iota(jnp.int32, (m_ref.shape[1], cols), 1)
    place = (s_i == (j_i // col_block) * col_sub + j_i % col_sub).astype(BF16)
    spread = _dot(m_ref[...].astype(BF16), place)
    r_g = (lax.broadcasted_iota(jnp.int32, (rows, cols), 0) // row_group) % gl
    c_g = (lax.broadcasted_iota(jnp.int32, (rows, cols), 1) % col_block) // col_sub
    o_ref[...] = jnp.where(r_g == c_g, spread, 0.0).astype(o_ref.dtype)


def _s5_expand(m, gl, row_group, col_block, col_sub):
    ng, rows, k = m.shape
    cols = k * gl
    return pl.pallas_call(
        functools.partial(_s5_expand_kernel, gl=gl, row_group=row_group, col_block=col_block, col_sub=col_sub),
        grid=(ng,),
        in_specs=[pl.BlockSpec((None, rows, k), lambda a: (a, 0, 0))],
        out_specs=pl.BlockSpec((None, rows, cols), lambda a: (a, 0, 0)),
        out_shape=jax.ShapeDtypeStruct((ng, rows, cols), BF16),
        compiler_params=_cparams("parallel"),
        name="s5_expand",
    )(m)


def _s5_core_kernel(u_ref, h0_ref, ws_ref, wi_ref, wh_ref, ap_ref, d_ref, g_ref, hout_ref,
                    s_scr, hs_scr, hcar, *, nb, nj):
    tc = S5_CHUNK
    rows = nb * nj
    nk = s_scr.shape[0]
    half = nk // 2
    lane_blk = lambda k: slice(k * LANES, (k + 1) * LANES)

    @pl.when(pl.program_id(1) == 0)
    def _():
        hcar[...] = h0_ref[...]

    def u_at(t):
        if nj == 1:
            return u_ref[:, t, :]
        return u_ref[:, pl.ds(t, nj, stride=tc), :].reshape(rows, LANES)

    us = [u_at(t) for t in range(tc)]
    x = jnp.concatenate([v.astype(BF16) for v in us], axis=1)
    s_all = _dot(x, ws_ref[...])
    for k in range(nk):
        s_scr[k] = s_all[:, lane_blk(k)]

    a_pow = [ap_ref[:, lane_blk(k)] for k in range(nk)]

    def rows_at(j):
        return slice(None) if nj == 1 else pl.ds(j, nb, stride=nj)

    def step(j, h):
        for k in range(nk):
            hs_scr[k, rows_at(j), :] = h[k]
        new_re, new_im = [], []
        for k in range(half):
            hr, hi, ar, ai = h[k], h[k + half], a_pow[k], a_pow[k + half]
            new_re.append(ar * hr - ai * hi + s_scr[k, rows_at(j), :])
            new_im.append(ar * hi + ai * hr + s_scr[k + half, rows_at(j), :])
        return tuple(new_re + new_im)

    h = tuple(hcar[:, lane_blk(k)] for k in range(nk))
    h = step(0, h) if nj == 1 else lax.fori_loop(0, nj, step, h)
    for k in range(nk):
        hcar[:, lane_blk(k)] = h[k]
        hout_ref[:, lane_blk(k)] = h[k]

    hs = jnp.concatenate([hs_scr[k].astype(BF16) for k in range(nk)], axis=1)
    y = _dot(x, wi_ref[...]) + _dot(hs, wh_ref[...])
    d = d_ref[...]
    for t in range(tc):
        gt = jax.nn.gelu(y[:, lane_blk(t)] + d * us[t])
        if nj == 1:
            g_ref[:, t, :] = gt
        else:
            g_ref[:, pl.ds(t, nj, stride=tc), :] = gt.reshape(nb, nj, LANES)


def _s5_core(u, h0, w_state, w_intra, w_carry, a_pow, d_skip):
    nb, nl, d = u.shape
    ng = d // LANES
    ts = min(nl, 512)
    nj = ts // S5_CHUNK
    sw = w_state.shape[2]
    kern = functools.partial(_s5_core_kernel, nb=nb, nj=nj)
    return pl.pallas_call(
        kern,
        grid=(ng, nl // ts),
        in_specs=[pl.BlockSpec((nb, ts, LANES), lambda g, s: (0, s, g)),
                  pl.BlockSpec((None, nb, sw), lambda g, s: (g, 0, 0)),
                  pl.BlockSpec((None,) + w_state.shape[1:], lambda g, s: (g, 0, 0)),
                  pl.BlockSpec((None,) + w_intra.shape[1:], lambda g, s: (g, 0, 0)),
                  pl.BlockSpec((None,) + w_carry.shape[1:], lambda g, s: (g, 0, 0)),
                  pl.BlockSpec((None, 1, sw), lambda g, s: (g, 0, 0)),
                  pl.BlockSpec((1, LANES), lambda g, s: (0, g))],
        out_specs=[pl.BlockSpec((nb, ts, LANES), lambda g, s: (0, s, g)),
                   pl.BlockSpec((None, nb, sw), lambda g, s: (g, 0, 0))],
        out_shape=[jax.ShapeDtypeStruct((nb, nl, d), F32),
                   jax.ShapeDtypeStruct((ng, nb, sw), F32)],
        scratch_shapes=[pltpu.VMEM((sw // LANES, nb * nj, LANES), F32),
                        pltpu.VMEM((sw // LANES, nb * nj, LANES), F32),
                        pltpu.VMEM((nb, sw), F32)],
        compiler_params=_cparams("parallel", "arbitrary"),
        name="s5_core",
    )(u, h0, w_state, w_intra, w_carry, a_pow, d_skip)


def _s5_state_in(re, im, ng):
    nb = re.shape[0]
    h = jnp.concatenate([re.reshape(nb, ng, -1), im.reshape(nb, ng, -1)], axis=-1)
    return h.transpose(1, 0, 2)


def _s5_state_out(h, g, p):
    ng, nb, sw = h.shape
    h = h.transpose(1, 0, 2)
    return h[..., :sw // 2].reshape(nb, g, p), h[..., sw // 2:].reshape(nb, g, p)


def _glu_out_kernel(g_ref, x_ref, wglu_ref, bglu_ref, wout_ref, gpost_ref, o_ref):
    g = g_ref[...]
    t = _dot(g.astype(BF16), wglu_ref[...]) + bglu_ref[...]
    z = (g * jax.nn.sigmoid(t)).astype(BF16)
    m = _dot(z, wout_ref[...])
    o_ref[...] = x_ref[...] + _rms(m, gpost_ref[...])


def _glu_out(g, x, w_glu, b_glu, w_out, g_post, tm):
    t, d = x.shape
    row = pl.BlockSpec((tm, d), lambda i: (i, 0))
    vec = pl.BlockSpec((1, d), lambda i: (0, 0))
    mat = pl.BlockSpec((d, d), lambda i: (0, 0))
    return pl.pallas_call(
        _glu_out_kernel,
        grid=(t // tm,),
        in_specs=[row, row, mat, vec, mat, vec],
        out_specs=row,
        out_shape=jax.ShapeDtypeStruct((t, d), F32),
        compiler_params=_cparams("parallel"),
        name="glu_out",
    )(g, x, w_glu, b_glu, w_out, g_post)


def _matmul_post_kernel(a1_ref, x1_ref, a2_ref, x2_ref, w_ref, gpost_ref, o_ref, *, n1):
    def emit(a_ref, x_ref):
        m = _dot(a_ref[...].astype(BF16), w_ref[...])
        o_ref[...] = x_ref[...] + _rms(m, gpost_ref[...])

    pl.when(pl.program_id(0) < n1)(lambda: emit(a1_ref, x1_ref))
    pl.when(pl.program_id(0) >= n1)(lambda: emit(a2_ref, x2_ref))


def _matmul_post_merged(a1, x1, a2, x2, w, g_post, tm):
    (t1, d), t2 = x1.shape, x2.shape[0]
    n1, n2 = t1 // tm, t2 // tm
    first = pl.BlockSpec((tm, d), lambda i: (jnp.minimum(i, n1 - 1), 0))
    second = pl.BlockSpec((tm, d), lambda i: (jnp.maximum(i - n1, 0), 0))
    return pl.pallas_call(
        functools.partial(_matmul_post_kernel, n1=n1),
        grid=(n1 + n2,),
        in_specs=[first, first, second, second,
                  pl.BlockSpec((d, d), lambda i: (0, 0)), pl.BlockSpec((1, d), lambda i: (0, 0))],
        out_specs=pl.BlockSpec((tm, d), lambda i: (i, 0)),
        out_shape=jax.ShapeDtypeStruct((t1 + t2, d), F32),
        compiler_params=_cparams("parallel"),
        name="matmul_post",
    )(a1, x1, a2, x2, w, g_post)


def _ffn_kernel(x_ref, gpre_ref, wg_ref, wu_ref, wd_ref, gpost_ref, o_ref, xn_scr, acc_scr):
    f, nf = pl.program_id(1), pl.num_programs(1)

    @pl.when(f == 0)
    def _():
        xn_scr[...] = _rms(x_ref[...], gpre_ref[...]).astype(BF16)
        acc_scr[...] = jnp.zeros_like(acc_scr)

    xn = xn_scr[...]
    h = (jax.nn.silu(_dot(xn, wg_ref[...])) * _dot(xn, wu_ref[...])).astype(BF16)
    acc_scr[...] += _dot(h, wd_ref[...])

    @pl.when(f == nf - 1)
    def _():
        o_ref[...] = x_ref[...] + _rms(acc_scr[...], gpost_ref[...])


def _ffn(x, g_pre, w_gate, w_up, w_down, g_post, tm, tf):
    t, d = x.shape
    ff = w_gate.shape[1]
    row = pl.BlockSpec((tm, d), lambda i, f: (i, 0))
    vec = pl.BlockSpec((1, d), lambda i, f: (0, 0))
    return pl.pallas_call(
        _ffn_kernel,
        grid=(t // tm, ff // tf),
        in_specs=[row, vec,
                  pl.BlockSpec((d, tf), lambda i, f: (0, f)),
                  pl.BlockSpec((d, tf), lambda i, f: (0, f)),
                  pl.BlockSpec((tf, d), lambda i, f: (f, 0)),
                  vec],
        out_specs=row,
        out_shape=jax.ShapeDtypeStruct((t, d), F32),
        scratch_shapes=[pltpu.VMEM((tm, d), BF16), pltpu.VMEM((tm, d), F32)],
        compiler_params=_cparams("parallel", "arbitrary"),
        name="ffn_dense",
    )(x, g_pre, w_gate, w_up, w_down, g_post)


META_G1, META_G2, META_E1, META_E2, META_R1, META_R2 = range(6)


def _router_kernel(x_ref, gpre_ref, wr_ref, br_ref, meta_ref, cnt_ref, cnt_scr):
    i = pl.program_id(0)

    @pl.when(i == 0)
    def _():
        cnt_scr[...] = jnp.zeros_like(cnt_scr)

    xn = _rms(x_ref[...], gpre_ref[...]).astype(BF16)
    logits = _dot(xn, wr_ref[...]) + br_ref[...]
    ex = jnp.exp(logits - jnp.max(logits, axis=-1, keepdims=True))
    probs = ex / jnp.sum(ex, axis=-1, keepdims=True)
    lane = lax.broadcasted_iota(jnp.int32, probs.shape, 1)
    p1 = jnp.max(probs, axis=-1, keepdims=True)
    i1 = jnp.min(jnp.where(probs == p1, lane, LANES), axis=-1, keepdims=True)
    rest = jnp.where(lane == i1, -1.0, probs)
    p2 = jnp.max(rest, axis=-1, keepdims=True)
    i2 = jnp.min(jnp.where(rest == p2, lane, LANES), axis=-1, keepdims=True)
    den = p1 + p2
    tm = probs.shape[0]
    chosen = (lane == i1) | (lane == i2)
    before = _tri01(tm, lambda r, c: c < r)
    seen = _dot(before, chosen.astype(BF16)) + cnt_scr[...]
    rank1 = jnp.sum(jnp.where(lane == i1, seen, 0.0), axis=-1, keepdims=True)
    rank2 = jnp.sum(jnp.where(lane == i2, seen, 0.0), axis=-1, keepdims=True)
    cnt_scr[...] += jnp.sum(chosen.astype(F32), axis=0, keepdims=True)
    cnt_ref[...] = cnt_scr[...]
    fields = {META_G1: p1 / den, META_G2: p2 / den, META_E1: i1.astype(F32), META_E2: i2.astype(F32),
              META_R1: rank1, META_R2: rank2}
    meta = jnp.zeros(probs.shape, F32)
    for k, v in fields.items():
        meta = jnp.where(lane == k, v, meta)
    meta_ref[...] = meta


def _router(x, g_pre, w_router, b_router, tm):
    t, d = x.shape
    ne = w_router.shape[1]
    wr = jnp.zeros((d, LANES), BF16).at[:, :ne].set(w_router.astype(BF16))
    br = jnp.full((1, LANES), -1e30, F32).at[0, :ne].set(b_router)
    return pl.pallas_call(
        _router_kernel,
        grid=(t // tm,),
        in_specs=[pl.BlockSpec((tm, d), lambda i: (i, 0)),
                  pl.BlockSpec((1, d), lambda i: (0, 0)),
                  pl.BlockSpec((d, LANES), lambda i: (0, 0)),
                  pl.BlockSpec((1, LANES), lambda i: (0, 0))],
        out_specs=[pl.BlockSpec((tm, LANES), lambda i: (i, 0)),
                   pl.BlockSpec((1, LANES), lambda i: (0, 0))],
        out_shape=[jax.ShapeDtypeStruct((t, LANES), F32), jax.ShapeDtypeStruct((1, LANES), F32)],
        scratch_shapes=[pltpu.VMEM((1, LANES), F32)],
        compiler_params=_cparams("arbitrary"),
        name="moe_router",
    )(x, g_pre, wr, br)


def _row_copy(src, src_row, dst, dst_row, sem):
    return pltpu.make_async_copy(src.at[pl.ds(src_row, 1)], dst.at[pl.ds(dst_row, 1)], sem)


def _dispatch_kernel(dst1_ref, dst2_ref, x_hbm, zeros_hbm, xs_hbm, sem, *, tr):
    del zeros_hbm
    base = pl.program_id(0) * tr

    def issue(r, _):
        t = base + r
        _row_copy(x_hbm, t, xs_hbm, dst1_ref[t], sem).start()
        _row_copy(x_hbm, t, xs_hbm, dst2_ref[t], sem).start()
        return 0

    lax.fori_loop(0, tr, issue, 0)

    def drain(r, _):
        t = base + r
        _row_copy(x_hbm, t, xs_hbm, dst1_ref[t], sem).wait()
        _row_copy(x_hbm, t, xs_hbm, dst2_ref[t], sem).wait()
        return 0

    lax.fori_loop(0, tr, drain, 0)


def _dispatch(x, dst1, dst2, n_slots, tr):
    t, d = x.shape
    grid_spec = pltpu.PrefetchScalarGridSpec(
        num_scalar_prefetch=2,
        grid=(t // tr,),
        in_specs=[pl.BlockSpec(memory_space=pl.ANY), pl.BlockSpec(memory_space=pl.ANY)],
        out_specs=pl.BlockSpec(memory_space=pl.ANY),
        scratch_shapes=[pltpu.SemaphoreType.DMA(())],
    )
    return pl.pallas_call(
        functools.partial(_dispatch_kernel, tr=tr),
        grid_spec=grid_spec,
        out_shape=jax.ShapeDtypeStruct((n_slots, d), F32),
        input_output_aliases={3: 0},
        compiler_params=_cparams("arbitrary"),
        name="moe_dispatch",
    )(dst1, dst2, x, jnp.zeros((n_slots, d), F32))


def _ffn_grouped_kernel(te_ref, tv_ref, x_ref, gpre_ref, wg_ref, wu_ref, wd_ref, y_ref, xn_scr):
    del te_ref
    i, f = pl.program_id(0), pl.program_id(1)
    valid = tv_ref[i] != 0

    @pl.when(f == 0)
    def _():
        xn_scr[...] = _rms(x_ref[...], gpre_ref[...]).astype(BF16)
        y_ref[...] = jnp.zeros_like(y_ref)

    @pl.when(valid)
    def _():
        xn = xn_scr[...]
        h = (jax.nn.silu(_dot(xn, wg_ref[...])) * _dot(xn, wu_ref[...])).astype(BF16)
        y_ref[...] += _dot(h, wd_ref[...])


def _ffn_grouped(xs, tile_expert, tile_valid, g_pre, w_gate, w_up, w_down, tm, tf):
    n_slots, d = xs.shape
    ff = w_gate.shape[2]
    row = pl.BlockSpec((tm, d), lambda i, f, te, tv: (i, 0))
    grid_spec = pltpu.PrefetchScalarGridSpec(
        num_scalar_prefetch=2,
        grid=(n_slots // tm, ff // tf),
        in_specs=[row,
                  pl.BlockSpec((1, d), lambda i, f, te, tv: (0, 0)),
                  pl.BlockSpec((None, d, tf), lambda i, f, te, tv: (te[i], 0, f)),
                  pl.BlockSpec((None, d, tf), lambda i, f, te, tv: (te[i], 0, f)),
                  pl.BlockSpec((None, tf, d), lambda i, f, te, tv: (te[i], f, 0))],
        out_specs=row,
        scratch_shapes=[pltpu.VMEM((tm, d), BF16)],
    )
    return pl.pallas_call(
        _ffn_grouped_kernel,
        grid_spec=grid_spec,
        out_shape=jax.ShapeDtypeStruct((n_slots, d), F32),
        compiler_params=_cparams("parallel", "arbitrary"),
        name="moe_ffn_grouped",
    )(tile_expert, tile_valid, xs, g_pre, w_gate, w_up, w_down)


def _combine_kernel(dst1_ref, dst2_ref, y_hbm, x_ref, meta_ref, gpost_ref, o_ref, ybuf, sems, *, tm, blk0):
    i, n = pl.program_id(0), pl.num_programs(0)

    def rows(step, slot, fn):
        base = (blk0 + step) * tm

        def body(r, _):
            t = base + r
            fn(_row_copy(y_hbm, dst1_ref[t], ybuf.at[slot, 0], r, sems.at[slot]))
            fn(_row_copy(y_hbm, dst2_ref[t], ybuf.at[slot, 1], r, sems.at[slot]))
            return 0

        lax.fori_loop(0, tm, body, 0)

    @pl.when(i == 0)
    def _():
        rows(0, 0, lambda c: c.start())

    for slot in range(2):
        @pl.when((i + 1 < n) & ((i + 1) % 2 == slot))
        def _(slot=slot):
            rows(i + 1, slot, lambda c: c.start())

    for slot in range(2):
        @pl.when(i % 2 == slot)
        def _(slot=slot):
            rows(i, slot, lambda c: c.wait())
            meta = meta_ref[...]
            g1 = meta[:, META_G1:META_G1 + 1]
            g2 = meta[:, META_G2:META_G2 + 1]
            mix = g1 * ybuf[slot, 0] + g2 * ybuf[slot, 1]
            o_ref[...] = x_ref[...] + _rms(mix, gpost_ref[...])


def _combine(y, x, meta, dst1, dst2, g_post, tm, row0, n_rows):
    d = x.shape[1]
    blk0 = row0 // tm
    grid_spec = pltpu.PrefetchScalarGridSpec(
        num_scalar_prefetch=2,
        grid=(n_rows // tm,),
        in_specs=[pl.BlockSpec(memory_space=pl.ANY),
                  pl.BlockSpec((tm, d), lambda i, d1, d2: (blk0 + i, 0)),
                  pl.BlockSpec((tm, LANES), lambda i, d1, d2: (blk0 + i, 0)),
                  pl.BlockSpec((1, d), lambda i, d1, d2: (0, 0))],
        out_specs=pl.BlockSpec((tm, d), lambda i, d1, d2: (i, 0)),
        scratch_shapes=[pltpu.VMEM((2, 2, tm, d), F32), pltpu.SemaphoreType.DMA((2,))],
    )
    return pl.pallas_call(
        functools.partial(_combine_kernel, tm=tm, blk0=blk0),
        grid_spec=grid_spec,
        out_shape=jax.ShapeDtypeStruct((n_rows, d), F32),
        compiler_params=_cparams("arbitrary"),
        name="moe_combine",
    )(dst1, dst2, y, x, meta, g_post)


def _moe_plan(meta, counts, n_experts, tm):
    t = meta.shape[0]
    cnt = counts[0, :n_experts].astype(jnp.int32)
    padded = ((cnt + tm - 1) // tm) * tm
    ends = jnp.cumsum(padded)
    offs = ends - padded
    e1, e2 = meta[:, META_E1].astype(jnp.int32), meta[:, META_E2].astype(jnp.int32)
    dst1 = offs[e1] + meta[:, META_R1].astype(jnp.int32)
    dst2 = offs[e2] + meta[:, META_R2].astype(jnp.int32)
    n_tiles = (2 * t + n_experts * (tm - 1)) // tm
    starts = jnp.arange(n_tiles, dtype=jnp.int32) * tm
    tile_expert = jnp.minimum(jnp.sum(starts[:, None] >= ends[None, :], axis=1), n_experts - 1).astype(jnp.int32)
    tile_valid = (starts < ends[-1]).astype(jnp.int32)
    return dst1, dst2, tile_expert, tile_valid, n_tiles * tm


def _log_sigmoid(z):
    return jnp.minimum(z, 0.0) - jnp.log(1.0 + jnp.exp(-jnp.abs(z)))


def _fox_proj_prompt_kernel(x_ref, gpre_ref, wq_ref, wkvt_ref, wft_ref, bf_ref,
                            q_ref, kt_ref, vt_ref, lft_ref, *, scale):
    xn = _rms(x_ref[...], gpre_ref[...]).astype(BF16)
    q_ref[...] = (_dot(xn, wq_ref[...]) * scale).astype(BF16)
    d = kt_ref.shape[0]
    kvt = _dot_nt(wkvt_ref[...], xn)
    kt_ref[...] = kvt[:d]
    vt_ref[...] = kvt[d:]
    lft_ref[...] = _log_sigmoid(_dot_nt(wft_ref[...], xn) + bf_ref[...])


def _fox_proj_prompt(x, g_pre, wq, wkvt, wft, b_f, scale, tm):
    nb, nl, d = x.shape
    nh = wft.shape[0]
    const = lambda b, i: (0, 0)
    return pl.pallas_call(
        functools.partial(_fox_proj_prompt_kernel, scale=scale),
        grid=(nb, nl // tm),
        in_specs=[pl.BlockSpec((None, tm, d), lambda b, i: (b, i, 0)),
                  pl.BlockSpec((1, d), const),
                  pl.BlockSpec(wq.shape, const),
                  pl.BlockSpec(wkvt.shape, const),
                  pl.BlockSpec(wft.shape, const),
                  pl.BlockSpec((nh, 1), const)],
        out_specs=[pl.BlockSpec((None, tm, d), lambda b, i: (b, i, 0)),
                   pl.BlockSpec((None, d, tm), lambda b, i: (b, 0, i)),
                   pl.BlockSpec((None, d, tm), lambda b, i: (b, 0, i)),
                   pl.BlockSpec((None, nh, tm), lambda b, i: (b, 0, i))],
        out_shape=[jax.ShapeDtypeStruct((nb, nl, d), BF16),
                   jax.ShapeDtypeStruct((nb, d, nl), F32),
                   jax.ShapeDtypeStruct((nb, d, nl), F32),
                   jax.ShapeDtypeStruct((nb, nh, nl), F32)],
        compiler_params=_cparams("parallel", "parallel"),
        name="fox_proj_prompt",
    )(x, g_pre, wq, wkvt, wft, b_f)


def _tri01(n, rel):
    r = lax.broadcasted_iota(jnp.int32, (n, n), 0)
    c = lax.broadcasted_iota(jnp.int32, (n, n), 1)
    return rel(r, c).astype(BF16)


def _cumsum_lanes_kernel(x_ref, o_ref):
    upper = _tri01(LANES, lambda r, c: r <= c)
    carry = jnp.zeros((x_ref.shape[0], 1), F32)
    for blk in range(x_ref.shape[1] // LANES):
        sl = slice(blk * LANES, (blk + 1) * LANES)
        cum = _dot_exact_rhs01(x_ref[:, sl], upper) + carry
        o_ref[:, sl] = cum
        carry = cum[:, LANES - 1:LANES]


def _cumsum_lanes(x):
    nb, r, nl = x.shape
    spec = pl.BlockSpec((None, r, nl), lambda b: (b, 0, 0))
    return pl.pallas_call(
        _cumsum_lanes_kernel, grid=(nb,), in_specs=[spec], out_specs=spec,
        out_shape=jax.ShapeDtypeStruct(x.shape, F32),
        compiler_params=_cparams("parallel"), name="logf_cumsum",
    )(x)


def _fox_attn_prompt_kernel(q_ref, kt_ref, vt_ref, c_ref, o_ref, *, tq, rb, hd):
    hpair, qi = pl.program_id(1), pl.program_id(2)
    heads = LANES // hd
    q0 = pl.multiple_of(qi * tq, tq)
    nrb = tq // rb
    row = lax.broadcasted_iota(jnp.int32, (rb, tq), 0)
    col = lax.broadcasted_iota(jnp.int32, (rb, tq), 1)
    hslice = [slice(hh * hd, (hh + 1) * hd) for hh in range(heads)]
    qs = [[q_ref[pl.ds(q0 + r * rb, rb), hs] for r in range(nrb)] for hs in hslice]

    def tile(k0, carry, masked):
        out = []
        for hh, hs in enumerate(hslice):
            kt = kt_ref[hs, pl.ds(k0, tq)].astype(BF16)
            vt = vt_ref[hs, pl.ds(k0, tq)].astype(BF16)
            cc = c_ref[pl.ds(hpair * heads + hh, 1), pl.ds(k0, tq)]
            for r in range(nrb):
                m, l, acc = carry[hh * nrb + r]
                s = _dot(qs[hh][r], kt) - cc
                if masked:
                    s = jnp.where(row + r * rb >= col, s, NEG_INF)
                m_new = jnp.maximum(m, jnp.max(s, axis=-1, keepdims=True))
                alpha = jnp.exp(m - m_new)
                p = jnp.exp(s - m_new)
                l = alpha * l + jnp.sum(p, axis=-1, keepdims=True)
                acc = alpha * acc + _dot_nt(p.astype(BF16), vt)
                out.append((m_new, l, acc))
        return tuple(out)

    init = tuple((jnp.full((rb, 1), NEG_INF, F32), jnp.zeros((rb, 1), F32), jnp.zeros((rb, hd), F32))
                 for _ in range(heads * nrb))
    carry = lax.fori_loop(0, qi, lambda j, cr: tile(pl.multiple_of(j * tq, tq), cr, False), init)
    carry = tile(q0, carry, True)
    for hh, hs in enumerate(hslice):
        for r in range(nrb):
            _, l, acc = carry[hh * nrb + r]
            o_ref[r * rb:(r + 1) * rb, hs] = (acc / l).astype(o_ref.dtype)


def _fox_attn_prompt(q, kt, vt, c, hd, tq):
    nb, nl, d = q.shape
    nh = c.shape[1]
    return pl.pallas_call(
        functools.partial(_fox_attn_prompt_kernel, tq=tq, rb=tq, hd=hd),
        grid=(nb, d // LANES, nl // tq),
        in_specs=[pl.BlockSpec((None, nl, LANES), lambda b, h, i: (b, 0, h)),
                  pl.BlockSpec((None, LANES, nl), lambda b, h, i: (b, h, 0)),
                  pl.BlockSpec((None, LANES, nl), lambda b, h, i: (b, h, 0)),
                  pl.BlockSpec((None, nh, nl), lambda b, h, i: (b, 0, 0))],
        out_specs=pl.BlockSpec((None, tq, LANES), lambda b, h, i: (b, i, h)),
        out_shape=jax.ShapeDtypeStruct((nb, nl, d), BF16),
        compiler_params=_cparams("parallel", "parallel", "arbitrary"),
        name="fox_attn_prompt",
    )(q, kt, vt, c)


def _fox_proj_sample_kernel(x_ref, gpre_ref, w_ref, wf_ref, bf_ref, q_ref, k_ref, v_ref, lf_ref, *, scale):
    xn = _rms(x_ref[...], gpre_ref[...]).astype(BF16)
    d = q_ref.shape[1]
    proj = _dot(xn, w_ref[...])
    q_ref[...] = proj[:, :d] * scale
    k_ref[...] = proj[:, d:2 * d]
    v_ref[...] = proj[:, 2 * d:]
    lf_ref[...] = _log_sigmoid(_dot(xn, wf_ref[...]) + bf_ref[...])


def _fox_proj_sample(x, g_pre, w_qkv, w_f, b_f, scale):
    t, d = x.shape
    nh = w_f.shape[1]
    wf = jnp.zeros((d, LANES), BF16).at[:, :nh].set(w_f)
    bf = jnp.zeros((1, LANES), F32).at[0, :nh].set(b_f)
    full = lambda a: pl.BlockSpec(a.shape, lambda i: (0,) * a.ndim)
    row = pl.BlockSpec((t, d), lambda i: (0, 0))
    return pl.pallas_call(
        functools.partial(_fox_proj_sample_kernel, scale=scale),
        grid=(1,),
        in_specs=[row, full(g_pre), full(w_qkv), full(wf), full(bf)],
        out_specs=[row, row, row, pl.BlockSpec((t, LANES), lambda i: (0, 0))],
        out_shape=[jax.ShapeDtypeStruct((t, d), F32)] * 3 + [jax.ShapeDtypeStruct((t, LANES), F32)],
        compiler_params=_cparams("arbitrary"),
        name="fox_proj_sample",
    )(x, g_pre, w_qkv, wf, bf)


def _paged_attn_kernel(tbl_ref, q_ref, *refs, ns, hd, gp):
    del tbl_ref
    kc_refs, vc_refs, lfc_refs = refs[:gp], refs[gp:2 * gp], refs[2 * gp:3 * gp]
    kn_ref, vn_ref, lfn_ref, o_ref, qbd, m_scr, l_scr, acc_scr, carry, bias_scr = refs[3 * gp:]
    p, n_chunks = pl.program_id(1), pl.num_programs(1)
    rows, d = qbd.shape
    nh = d // hd
    r_i = lax.broadcasted_iota(jnp.int32, (rows, d), 0)
    c_i = lax.broadcasted_iota(jnp.int32, (rows, d), 1)
    bmask = (r_i // ns) == (c_i // hd)
    expand = (lax.broadcasted_iota(jnp.int32, (rows, nh), 0) // ns
              == lax.broadcasted_iota(jnp.int32, (rows, nh), 1)).astype(BF16)

    @pl.when(p == 0)
    def _():
        qt = jnp.concatenate([q_ref[...]] * nh, axis=0)
        qbd[...] = jnp.where(bmask, qt, 0.0).astype(BF16)
        m_scr[...] = jnp.full_like(m_scr, NEG_INF)
        l_scr[...] = jnp.zeros_like(l_scr)
        acc_scr[...] = jnp.zeros_like(acc_scr)
        carry[...] = jnp.zeros_like(carry)

    def update(s, v_nt=None, v_nn=None):
        m_new = jnp.maximum(m_scr[...], jnp.max(s, axis=-1, keepdims=True))
        alpha = jnp.exp(m_scr[...] - m_new)
        pr = jnp.exp(s - m_new)
        l_scr[...] = alpha * l_scr[...] + jnp.sum(pr, axis=-1, keepdims=True)
        pv = _dot_nt(pr.astype(BF16), v_nt) if v_nt is not None else _dot(pr.astype(BF16), v_nn)
        acc_scr[...] = alpha * acc_scr[...] + pv
        m_scr[...] = m_new

    lfs = [r[...] for r in lfc_refs]
    after = _tri01(LANES, lambda r, c: r > c)
    within = _dot_exact_rhs01(jnp.concatenate(lfs, axis=0), after)
    run = carry[...]
    for k in reversed(range(gp)):
        w_k = within[k * nh:(k + 1) * nh]
        bias_scr[:, k * LANES:(k + 1) * LANES] = run + w_k
        run = run + (w_k[:, 0:1] + lfs[k][:, 0:1])
    carry[...] = run
    kcat = jnp.concatenate([r[...].astype(BF16) for r in kc_refs], axis=1)
    vcat = jnp.concatenate([r[...].astype(BF16) for r in vc_refs], axis=1)
    s = _dot(qbd[...], kcat)
    s = jnp.concatenate([s[h * ns:(h + 1) * ns] + bias_scr[h:h + 1, :] for h in range(nh)], axis=0)
    update(s, v_nt=vcat)

    @pl.when(p == n_chunks - 1)
    def _():
        upto = _tri01(LANES, lambda r, c: r <= c)
        cn = _dot_exact_rhs01(lfn_ref[...], upto)
        pad = jnp.zeros((LANES - ns, d), F32)
        kn = jnp.concatenate([kn_ref[...], pad], axis=0).astype(BF16)
        vn = jnp.concatenate([vn_ref[...], pad], axis=0).astype(BF16)
        sn = _dot_nt(qbd[...], kn) - _dot_exact_lhs01(expand, cn)
        qpos = lax.broadcasted_iota(jnp.int32, (rows, LANES), 0) % ns
        kpos = lax.broadcasted_iota(jnp.int32, (rows, LANES), 1)
        update(jnp.where(kpos <= qpos, sn, NEG_INF), v_nn=vn)
        o = jnp.where(bmask, acc_scr[...] / l_scr[...], 0.0)
        out = o[0:ns]
        for h in range(1, nh):
            out = out + o[h * ns:(h + 1) * ns]
        o_ref[...] = out


def _paged_attn(page_table, q, kc, vc, lfc, kn, vn, lfn, hd):
    db, ns, d = q.shape
    n_pages = page_table.shape[1]
    nh = d // hd
    rows = nh * ns
    page = kc.shape[2]
    assert rows == LANES and page == LANES
    gp = next(g for g in (8, 4, 2, 1) if n_pages % g == 0)
    n_chunks = n_pages // gp
    seq = pl.BlockSpec((None, ns, d), lambda b, p, tbl: (b, 0, 0))

    def pg(k):
        return lambda b, p, tbl: (tbl[b, (n_chunks - 1 - p) * gp + k], 0, 0)

    grid_spec = pltpu.PrefetchScalarGridSpec(
        num_scalar_prefetch=1,
        grid=(db, n_chunks),
        in_specs=([seq]
                  + [pl.BlockSpec((None, d, page), pg(k)) for k in range(gp)]
                  + [pl.BlockSpec((None, d, page), pg(k)) for k in range(gp)]
                  + [pl.BlockSpec((None, nh, page), pg(k)) for k in range(gp)]
                  + [seq, seq, pl.BlockSpec((None, nh, LANES), lambda b, p, tbl: (b, 0, 0))]),
        out_specs=seq,
        scratch_shapes=[pltpu.VMEM((rows, d), BF16),
                        pltpu.VMEM((rows, 1), F32),
                        pltpu.VMEM((rows, 1), F32),
                        pltpu.VMEM((rows, d), F32),
                        pltpu.VMEM((nh, 1), F32),
                        pltpu.VMEM((nh, gp * page), F32)],
    )
    return pl.pallas_call(
        functools.partial(_paged_attn_kernel, ns=ns, hd=hd, gp=gp),
        grid_spec=grid_spec,
        out_shape=jax.ShapeDtypeStruct((db, ns, d), F32),
        compiler_params=_cparams("parallel", "arbitrary"),
        name="fox_paged_attn",
    )(page_table, q, *([kc] * gp), *([vc] * gp), *([lfc] * gp), kn, vn, lfn)


def _div_tile(t, candidates):
    return next((tm for tm in candidates if t % tm == 0), t)


def _row_tile(t):
    return _div_tile(t, (1024, 512, 256, 128))


def kernel(x_prompt, x_sample, state_ssm_re, state_ssm_im, cache_k, cache_v, cache_logf, page_table, norm_mix_pre, norm_mix_post, norm_ffn_pre, norm_ffn_post, ssm_w_in, ssm_lambda_re, ssm_lambda_im, ssm_log_dt, ssm_b_re, ssm_b_im, ssm_c_re, ssm_c_im, ssm_d, ssm_w_glu, ssm_b_glu, ssm_w_out, fox_w_in, fox_b_f, fox_w_out, ffn_w_gate, ffn_w_up, ffn_w_down, moe_w_router, moe_b_router, moe_w_gate, moe_w_up, moe_w_down):
    nb, nl, d = x_prompt.shape
    db, ns, _ = x_sample.shape
    n_groups, n_state = ssm_lambda_re.shape[1:]
    nh = fox_b_f.shape[1]
    hd = d // nh
    scale = hd ** -0.5
    ng = d // LANES
    streams = [x_prompt.reshape(nb * nl, d), x_sample.reshape(db * ns, d)]
    dims = [(nb, nl), (db, ns)]
    tms = [_row_tile(x.shape[0]) for x in streams]
    vec = lambda a: a.reshape(1, -1)

    li = 0
    w_in = ssm_w_in[li].astype(BF16)
    w_glu = ssm_w_glu[li].astype(BF16)
    w_out = ssm_w_out[li].astype(BF16)
    s5w = _s5_prepare(ssm_lambda_re[li], ssm_lambda_im[li], ssm_log_dt[li], ssm_b_re[li], ssm_b_im[li],
                      ssm_c_re[li], ssm_c_im[li])
    wg, wu, wd = (w[li].astype(BF16) for w in (ffn_w_gate, ffn_w_up, ffn_w_down))
    h0s = [jnp.zeros((ng, nb, 2 * (n_groups // ng) * n_state), F32),
           _s5_state_in(state_ssm_re[li], state_ssm_im[li], ng)]
    ssm_states = []
    for si in range(2):
        x, (b_, l_), tm = streams[si], dims[si], tms[si]
        u = _norm_matmul(x, vec(norm_mix_pre[0]), w_in, tm)
        g_act, h_fin = _s5_core(u.reshape(b_, l_, d), h0s[si], *s5w, vec(ssm_d[li]))
        ssm_states.append(_s5_state_out(h_fin, n_groups, n_state))
        x = _glu_out(g_act.reshape(b_ * l_, d), x, w_glu, vec(ssm_b_glu[li]), w_out, vec(norm_mix_post[0]), tm)
        streams[si] = _ffn(x, vec(norm_ffn_pre[0]), wg, wu, wd, vec(norm_ffn_post[0]), tm, 512)

    w_fox = fox_w_in[li]
    w_fox_t = w_fox.T
    wq = w_fox[:, :d].astype(BF16)
    wkvt = w_fox_t[d:3 * d].astype(BF16)
    wft = w_fox_t[3 * d:].astype(BF16)
    w_o = fox_w_out[li].astype(BF16)

    xp = streams[0]
    q, kt, vt, lft = _fox_proj_prompt(xp.reshape(nb, nl, d), vec(norm_mix_pre[1]), wq, wkvt, wft,
                                      fox_b_f[li].reshape(nh, 1), scale, min(nl, 1024))
    c = _cumsum_lanes(lft)
    o = _fox_attn_prompt(q, kt, vt, c, hd, min(nl, 512))
    tp, tsm = nb * nl, db * ns
    k_prompt = kt.reshape(nb, nh, hd, nl).transpose(0, 3, 1, 2)[None]
    v_prompt = vt.reshape(nb, nh, hd, nl).transpose(0, 3, 1, 2)[None]
    logf_prompt = lft.transpose(0, 2, 1)[None]

    xs = streams[1]
    qs, ks, vs, lfs = _fox_proj_sample(xs, vec(norm_mix_pre[1]), w_fox[:, :3 * d].astype(BF16),
                                       w_fox[:, 3 * d:].astype(BF16), fox_b_f[li], scale)
    n_pool, page = cache_k.shape[1], cache_k.shape[2]
    kc = cache_k[li].transpose(0, 2, 3, 1).reshape(n_pool, d, page)
    vc = cache_v[li].transpose(0, 2, 3, 1).reshape(n_pool, d, page)
    lfc = cache_logf[li].transpose(0, 2, 1)
    lfs = lfs[:, :nh]
    lfn = jnp.zeros((db, nh, LANES), F32).at[:, :, :ns].set(lfs.reshape(db, ns, nh).transpose(0, 2, 1))
    os_ = _paged_attn(page_table, qs.reshape(db, ns, d), kc, vc, lfc,
                      ks.reshape(db, ns, d), vs.reshape(db, ns, d), lfn, hd)
    small = (256, 128, 64, 32, 16, 8)
    x_all = _matmul_post_merged(o.reshape(tp, d), xp, os_.reshape(tsm, d), xs, w_o, vec(norm_mix_post[1]),
                                _div_tile(tsm, small))
    k_sample = ks.reshape(1, db, ns, nh, hd)
    v_sample = vs.reshape(1, db, ns, nh, hd)
    logf_sample = lfs.reshape(1, db, ns, nh)

    mg, mu, md = (w[li].astype(BF16) for w in (moe_w_gate, moe_w_up, moe_w_down))
    n_experts = mg.shape[0]
    t_route = _div_tile(tp + tsm, small)
    meta, counts = _router(x_all, vec(norm_ffn_pre[1]), moe_w_router[li], moe_b_router[li], t_route)
    dst1, dst2, tile_expert, tile_valid, n_slots = _moe_plan(meta, counts, n_experts, MOE_ROW_TILE)
    x_sorted = _dispatch(x_all, dst1, dst2, n_slots, t_route)
    y_sorted = _ffn_grouped(x_sorted, tile_expert, tile_valid, vec(norm_ffn_pre[1]), mg, mu, md, MOE_ROW_TILE, 512)
    outs = [_combine(y_sorted, x_all, meta, dst1, dst2, vec(norm_ffn_post[1]), _div_tile(n, small), r0, n)
            for r0, n in ((0, tp), (tp, tsm))]

    (re_p, im_p), (re_s, im_s) = ssm_states
    return (outs[0].reshape(nb, nl, d), outs[1].reshape(db, ns, d),
            re_p[None], im_p[None], re_s[None], im_s[None],
            k_prompt, v_prompt, logf_prompt, k_sample, v_sample, logf_sample)
```

```python
import functools

import jax
import jax.numpy as jnp
from jax import lax
from jax.experimental import pallas as pl
from jax.experimental.pallas import tpu as pltpu

F32 = jnp.float32
BF16 = jnp.bfloat16
RMS_EPS = 1e-6
LANES = 128
SUBLANES = 8
V7X_VMEM_LIMIT_BYTES = 56 << 20
S5_CHUNK = 8
MOE_ROW_TILE = 512
DMA_LOOP_UNROLL = 8
NEG_INF = float("-inf")


def _cparams(*sem):
    return pltpu.CompilerParams(dimension_semantics=sem, vmem_limit_bytes=V7X_VMEM_LIMIT_BYTES)


def _rms(x, g):
    return x * lax.rsqrt(jnp.mean(x * x, axis=-1, keepdims=True) + RMS_EPS) * g


def _dot(a, b):
    return jnp.dot(a, b, preferred_element_type=F32)


def _dot_nt(a, b):
    return lax.dot_general(a, b, (((1,), (1,)), ((), ())), preferred_element_type=F32)


def _split3(x):
    hi = x.astype(BF16)
    r = x - hi.astype(F32)
    mid = r.astype(BF16)
    lo = (r - mid.astype(F32)).astype(BF16)
    return hi, mid, lo


def _dot_exact_rhs01(x, m01):
    hi, mid, lo = _split3(x)
    return _dot(hi, m01) + _dot(mid, m01) + _dot(lo, m01)


def _dot_exact_lhs01(m01, x):
    hi, mid, lo = _split3(x)
    return _dot(m01, hi) + _dot(m01, mid) + _dot(m01, lo)


def _norm_matmul_kernel(x_ref, g_ref, w_ref, o_ref):
    xn = _rms(x_ref[...], g_ref[...]).astype(BF16)
    o_ref[...] = _dot(xn, w_ref[...])


def _norm_matmul(x, g, w, tm):
    t, d = x.shape
    n = w.shape[1]
    return pl.pallas_call(
        _norm_matmul_kernel,
        grid=(t // tm,),
        in_specs=[pl.BlockSpec((tm, d), lambda i: (i, 0)),
                  pl.BlockSpec((1, d), lambda i: (0, 0)),
                  pl.BlockSpec((d, n), lambda i: (0, 0))],
        out_specs=pl.BlockSpec((tm, n), lambda i: (i, 0)),
        out_shape=jax.ShapeDtypeStruct((t, n), F32),
        compiler_params=_cparams("parallel"),
        name="norm_matmul",
    )(x, g, w)


def _s5_prepare(lam_re, lam_im, log_dt, b_re, b_im, c_re, c_im):
    tc = S5_CHUNK
    g, p = lam_re.shape
    c = b_re.shape[-1]
    ng = (g * c) // LANES
    gl = g // ng
    hp = lax.Precision.HIGHEST
    dt = jnp.exp(log_dt)[:, None]
    mag = jnp.exp(lam_re * dt)
    ar = mag * jnp.cos(lam_im * dt)
    ai = mag * jnp.sin(lam_im * dt)
    den = lam_re * lam_re + lam_im * lam_im
    nr = ar - 1.0
    coef_r = (nr * lam_re + ai * lam_im) / den
    coef_i = (ai * lam_re - nr * lam_im) / den
    bbr = coef_r[..., None] * b_re - coef_i[..., None] * b_im
    bbi = coef_r[..., None] * b_im + coef_i[..., None] * b_re
    prs, pis = [jnp.ones_like(ar)], [jnp.zeros_like(ar)]
    for _ in range(tc):
        nr_, ni_ = prs[-1] * ar - pis[-1] * ai, prs[-1] * ai + pis[-1] * ar
        prs.append(nr_)
        pis.append(ni_)
    pr = jnp.stack(prs)
    pi = jnp.stack(pis)
    prs_ = pr[tc - 1 - jnp.arange(tc)][..., None]
    pis_ = pi[tc - 1 - jnp.arange(tc)][..., None]
    ms = jnp.stack([prs_ * bbr - pis_ * bbi, prs_ * bbi + pis_ * bbr])
    ms = ms.transpose(2, 1, 4, 0, 3).reshape(ng, gl, tc, c, 2 * p)
    m_state = ms.transpose(0, 2, 1, 3, 4).reshape(ng, tc * gl * c, 2 * p)

    er = pr[:tc, :, :, None] * bbr - pi[:tc, :, :, None] * bbi
    ei = pr[:tc, :, :, None] * bbi + pi[:tc, :, :, None] * bbr
    kk = (jnp.einsum("gdp,kgpc->kgdc", c_re, er, precision=hp)
          - jnp.einsum("gdp,kgpc->kgdc", c_im, ei, precision=hp))
    lag = jnp.arange(tc)[None, :] - jnp.arange(tc)[:, None]
    kt = jnp.where((lag >= 0)[:, :, None, None, None], kk[jnp.clip(lag, 0)], 0.0)
    kt = kt.transpose(2, 0, 4, 1, 3).reshape(ng, gl, tc, c, tc * c)
    m_intra = kt.transpose(0, 2, 1, 3, 4).reshape(ng, tc * gl * c, tc * c)

    pr1 = pr[1:].transpose(1, 2, 0)[:, :, :, None]
    pi1 = pi[1:].transpose(1, 2, 0)[:, :, :, None]
    cre = c_re.transpose(0, 2, 1)[:, :, None, :]
    cim = c_im.transpose(0, 2, 1)[:, :, None, :]
    hh = jnp.stack([cre * pr1 - cim * pi1, -(cre * pi1 + cim * pr1)])
    m_carry = hh.reshape(2, ng, gl * p, tc * c).transpose(1, 0, 2, 3).reshape(ng, 2 * gl * p, tc * c)

    w_state = _s5_expand(m_state, gl, row_group=c, col_block=gl * p, col_sub=p)
    w_intra = _s5_expand(m_intra, gl, row_group=c, col_block=gl * c, col_sub=c)
    w_carry = _s5_expand(m_carry, gl, row_group=p, col_block=gl * c, col_sub=c)
    a_pow = jnp.concatenate([pr[tc].reshape(ng, 1, gl * p), pi[tc].reshape(ng, 1, gl * p)], axis=-1)
    return w_state, w_intra, w_carry, a_pow


def _s5_expand_kernel(m_ref, o_ref, *, gl, row_group, col_block, col_sub):
    rows, cols = o_ref.shape
    s_i = lax.broadcasted_iota(jnp.int32, (m_ref.shape[1], cols), 0)
    j_i = lax.broadcasted_iota(jnp.int32, (m_ref.shape[1], cols), 1)
    place = (s_i == (j_i // col_block) * col_sub + j_i % col_sub).astype(BF16)
    spread = _dot(m_ref[...].astype(BF16), place)
    r_g = (lax.broadcasted_iota(jnp.int32, (rows, cols), 0) // row_group) % gl
    c_g = (lax.broadcasted_iota(jnp.int32, (rows, cols), 1) % col_block) // col_sub
    o_ref[...] = jnp.where(r_g == c_g, spread, 0.0).astype(o_ref.dtype)


def _s5_expand(m, gl, row_group, col_block, col_sub):
    ng, rows, k = m.shape
    cols = k * gl
    return pl.pallas_call(
        functools.partial(_s5_expand_kernel, gl=gl, row_group=row_group, col_block=col_block, col_sub=col_sub),
        grid=(ng,),
        in_specs=[pl.BlockSpec((None, rows, k), lambda a: (a, 0, 0))],
        out_specs=pl.BlockSpec((None, rows, cols), lambda a: (a, 0, 0)),
        out_shape=jax.ShapeDtypeStruct((ng, rows, cols), BF16),
        compiler_params=_cparams("parallel"),
        name="s5_expand",
    )(m)


def _s5_core_kernel(u_ref, h0_ref, ws_ref, wi_ref, wh_ref, ap_ref, d_ref, g_ref, hout_ref,
                    s_scr, hs_scr, hcar, *, nb, nj):
    tc = S5_CHUNK
    rows = nb * nj
    nk = s_scr.shape[0]
    half = nk // 2
    lane_blk = lambda k: slice(k * LANES, (k + 1) * LANES)

    @pl.when(pl.program_id(1) == 0)
    def _():
        hcar[...] = h0_ref[...]

    def u_at(t):
        if nj == 1:
            return u_ref[:, t, :]
        return u_ref[:, pl.ds(t, nj, stride=tc), :].reshape(rows, LANES)

    us = [u_at(t) for t in range(tc)]
    x = jnp.concatenate([v.astype(BF16) for v in us], axis=1)
    s_all = _dot(x, ws_ref[...])
    for k in range(nk):
        s_scr[k] = s_all[:, lane_blk(k)]

    a_pow = [ap_ref[:, lane_blk(k)] for k in range(nk)]

    def rows_at(j):
        return slice(None) if nj == 1 else pl.ds(j, nb, stride=nj)

    def step(j, h):
        for k in range(nk):
            hs_scr[k, rows_at(j), :] = h[k]
        new_re, new_im = [], []
        for k in range(half):
            hr, hi, ar, ai = h[k], h[k + half], a_pow[k], a_pow[k + half]
            new_re.append(ar * hr - ai * hi + s_scr[k, rows_at(j), :])
            new_im.append(ar * hi + ai * hr + s_scr[k + half, rows_at(j), :])
        return tuple(new_re + new_im)

    h = tuple(hcar[:, lane_blk(k)] for k in range(nk))
    h = step(0, h) if nj == 1 else lax.fori_loop(0, nj, step, h)
    for k in range(nk):
        hcar[:, lane_blk(k)] = h[k]
        hout_ref[:, lane_blk(k)] = h[k]

    hs = jnp.concatenate([hs_scr[k].astype(BF16) for k in range(nk)], axis=1)
    y = _dot(x, wi_ref[...]) + _dot(hs, wh_ref[...])
    d = d_ref[...]
    for t in range(tc):
        gt = jax.nn.gelu(y[:, lane_blk(t)] + d * us[t])
        if nj == 1:
            g_ref[:, t, :] = gt
        else:
            g_ref[:, pl.ds(t, nj, stride=tc), :] = gt.reshape(nb, nj, LANES)


def _s5_core(u, h0, w_state, w_intra, w_carry, a_pow, d_skip):
    nb, nl, d = u.shape
    ng = d // LANES
    ts = min(nl, 512)
    nj = ts // S5_CHUNK
    sw = w_state.shape[2]
    kern = functools.partial(_s5_core_kernel, nb=nb, nj=nj)
    return pl.pallas_call(
        kern,
        grid=(ng, nl // ts),
        in_specs=[pl.BlockSpec((nb, ts, LANES), lambda g, s: (0, s, g)),
                  pl.BlockSpec((None, nb, sw), lambda g, s: (g, 0, 0)),
                  pl.BlockSpec((None,) + w_state.shape[1:], lambda g, s: (g, 0, 0)),
                  pl.BlockSpec((None,) + w_intra.shape[1:], lambda g, s: (g, 0, 0)),
                  pl.BlockSpec((None,) + w_carry.shape[1:], lambda g, s: (g, 0, 0)),
                  pl.BlockSpec((None, 1, sw), lambda g, s: (g, 0, 0)),
                  pl.BlockSpec((1, LANES), lambda g, s: (0, g))],
        out_specs=[pl.BlockSpec((nb, ts, LANES), lambda g, s: (0, s, g)),
                   pl.BlockSpec((None, nb, sw), lambda g, s: (g, 0, 0))],
        out_shape=[jax.ShapeDtypeStruct((nb, nl, d), F32),
                   jax.ShapeDtypeStruct((ng, nb, sw), F32)],
        scratch_shapes=[pltpu.VMEM((sw // LANES, nb * nj, LANES), F32),
                        pltpu.VMEM((sw // LANES, nb * nj, LANES), F32),
                        pltpu.VMEM((nb, sw), F32)],
        compiler_params=_cparams("parallel", "arbitrary"),
        name="s5_core",
    )(u, h0, w_state, w_intra, w_carry, a_pow, d_skip)


def _s5_state_in(re, im, ng):
    nb = re.shape[0]
    h = jnp.concatenate([re.reshape(nb, ng, -1), im.reshape(nb, ng, -1)], axis=-1)
    return h.transpose(1, 0, 2)


def _s5_state_out(h, g, p):
    ng, nb, sw = h.shape
    h = h.transpose(1, 0, 2)
    return h[..., :sw // 2].reshape(nb, g, p), h[..., sw // 2:].reshape(nb, g, p)


def _glu_out_kernel(g_ref, x_ref, wglu_ref, bglu_ref, wout_ref, gpost_ref, o_ref):
    g = g_ref[...]
    t = _dot(g.astype(BF16), wglu_ref[...]) + bglu_ref[...]
    z = (g * jax.nn.sigmoid(t)).astype(BF16)
    m = _dot(z, wout_ref[...])
    o_ref[...] = x_ref[...] + _rms(m, gpost_ref[...])


def _glu_out(g, x, w_glu, b_glu, w_out, g_post, tm):
    t, d = x.shape
    row = pl.BlockSpec((tm, d), lambda i: (i, 0))
    vec = pl.BlockSpec((1, d), lambda i: (0, 0))
    mat = pl.BlockSpec((d, d), lambda i: (0, 0))
    return pl.pallas_call(
        _glu_out_kernel,
        grid=(t // tm,),
        in_specs=[row, row, mat, vec, mat, vec],
        out_specs=row,
        out_shape=jax.ShapeDtypeStruct((t, d), F32),
        compiler_params=_cparams("parallel"),
        name="glu_out",
    )(g, x, w_glu, b_glu, w_out, g_post)


def _matmul_post_kernel(a1_ref, x1_ref, a2_ref, x2_ref, w_ref, gpost_ref, o_ref, *, n1):
    def emit(a_ref, x_ref):
        m = _dot(a_ref[...].astype(BF16), w_ref[...])
        o_ref[...] = x_ref[...] + _rms(m, gpost_ref[...])

    pl.when(pl.program_id(0) < n1)(lambda: emit(a1_ref, x1_ref))
    pl.when(pl.program_id(0) >= n1)(lambda: emit(a2_ref, x2_ref))


def _matmul_post_merged(a1, x1, a2, x2, w, g_post, tm):
    (t1, d), t2 = x1.shape, x2.shape[0]
    n1, n2 = t1 // tm, t2 // tm
    first = pl.BlockSpec((tm, d), lambda i: (jnp.minimum(i, n1 - 1), 0))
    second = pl.BlockSpec((tm, d), lambda i: (jnp.maximum(i - n1, 0), 0))
    return pl.pallas_call(
        functools.partial(_matmul_post_kernel, n1=n1),
        grid=(n1 + n2,),
        in_specs=[first, first, second, second,
                  pl.BlockSpec((d, d), lambda i: (0, 0)), pl.BlockSpec((1, d), lambda i: (0, 0))],
        out_specs=pl.BlockSpec((tm, d), lambda i: (i, 0)),
        out_shape=jax.ShapeDtypeStruct((t1 + t2, d), F32),
        compiler_params=_cparams("parallel"),
        name="matmul_post",
    )(a1, x1, a2, x2, w, g_post)


def _ffn_kernel(x_ref, gpre_ref, wg_ref, wu_ref, wd_ref, gpost_ref, o_ref, xn_scr, acc_scr):
    f, nf = pl.program_id(1), pl.num_programs(1)

    @pl.when(f == 0)
    def _():
        xn_scr[...] = _rms(x_ref[...], gpre_ref[...]).astype(BF16)
        acc_scr[...] = jnp.zeros_like(acc_scr)

    xn = xn_scr[...]
    h = (jax.nn.silu(_dot(xn, wg_ref[...])) * _dot(xn, wu_ref[...])).astype(BF16)
    acc_scr[...] += _dot(h, wd_ref[...])

    @pl.when(f == nf - 1)
    def _():
        o_ref[...] = x_ref[...] + _rms(acc_scr[...], gpost_ref[...])


def _ffn(x, g_pre, w_gate, w_up, w_down, g_post, tm, tf):
    t, d = x.shape
    ff = w_gate.shape[1]
    row = pl.BlockSpec((tm, d), lambda i, f: (i, 0))
    vec = pl.BlockSpec((1, d), lambda i, f: (0, 0))
    return pl.pallas_call(
        _ffn_kernel,
        grid=(t // tm, ff // tf),
        in_specs=[row, vec,
                  pl.BlockSpec((d, tf), lambda i, f: (0, f)),
                  pl.BlockSpec((d, tf), lambda i, f: (0, f)),
                  pl.BlockSpec((tf, d), lambda i, f: (f, 0)),
                  vec],
        out_specs=row,
        out_shape=jax.ShapeDtypeStruct((t, d), F32),
        scratch_shapes=[pltpu.VMEM((tm, d), BF16), pltpu.VMEM((tm, d), F32)],
        compiler_params=_cparams("parallel", "arbitrary"),
        name="ffn_dense",
    )(x, g_pre, w_gate, w_up, w_down, g_post)


META_G1, META_G2, META_E1, META_E2, META_R1, META_R2 = range(6)


def _router_kernel(x_ref, gpre_ref, wr_ref, br_ref, meta_ref, cnt_ref, cnt_scr):
    i = pl.program_id(0)

    @pl.when(i == 0)
    def _():
        cnt_scr[...] = jnp.zeros_like(cnt_scr)

    xn = _rms(x_ref[...], gpre_ref[...]).astype(BF16)
    logits = _dot(xn, wr_ref[...]) + br_ref[...]
    ex = jnp.exp(logits - jnp.max(logits, axis=-1, keepdims=True))
    probs = ex / jnp.sum(ex, axis=-1, keepdims=True)
    lane = lax.broadcasted_iota(jnp.int32, probs.shape, 1)
    p1 = jnp.max(probs, axis=-1, keepdims=True)
    i1 = jnp.min(jnp.where(probs == p1, lane, LANES), axis=-1, keepdims=True)
    rest = jnp.where(lane == i1, -1.0, probs)
    p2 = jnp.max(rest, axis=-1, keepdims=True)
    i2 = jnp.min(jnp.where(rest == p2, lane, LANES), axis=-1, keepdims=True)
    den = p1 + p2
    tm = probs.shape[0]
    chosen = (lane == i1) | (lane == i2)
    before = _tri01(tm, lambda r, c: c < r)
    seen = _dot(before, chosen.astype(BF16)) + cnt_scr[...]
    rank1 = jnp.sum(jnp.where(lane == i1, seen, 0.0), axis=-1, keepdims=True)
    rank2 = jnp.sum(jnp.where(lane == i2, seen, 0.0), axis=-1, keepdims=True)
    cnt_scr[...] += jnp.sum(chosen.astype(F32), axis=0, keepdims=True)
    cnt_ref[...] = cnt_scr[...]
    fields = {META_G1: p1 / den, META_G2: p2 / den, META_E1: i1.astype(F32), META_E2: i2.astype(F32),
              META_R1: rank1, META_R2: rank2}
    meta = jnp.zeros(probs.shape, F32)
    for k, v in fields.items():
        meta = jnp.where(lane == k, v, meta)
    meta_ref[...] = meta


def _router(x, g_pre, w_router, b_router, tm):
    t, d = x.shape
    ne = w_router.shape[1]
    wr = jnp.zeros((d, LANES), BF16).at[:, :ne].set(w_router.astype(BF16))
    br = jnp.full((1, LANES), -1e30, F32).at[0, :ne].set(b_router)
    return pl.pallas_call(
        _router_kernel,
        grid=(t // tm,),
        in_specs=[pl.BlockSpec((tm, d), lambda i: (i, 0)),
                  pl.BlockSpec((1, d), lambda i: (0, 0)),
                  pl.BlockSpec((d, LANES), lambda i: (0, 0)),
                  pl.BlockSpec((1, LANES), lambda i: (0, 0))],
        out_specs=[pl.BlockSpec((tm, LANES), lambda i: (i, 0)),
                   pl.BlockSpec((1, LANES), lambda i: (0, 0))],
        out_shape=[jax.ShapeDtypeStruct((t, LANES), F32), jax.ShapeDtypeStruct((1, LANES), F32)],
        scratch_shapes=[pltpu.VMEM((1, LANES), F32)],
        compiler_params=_cparams("arbitrary"),
        name="moe_router",
    )(x, g_pre, wr, br)


def _row_copy(src, src_row, dst, dst_row, sem):
    return pltpu.make_async_copy(src.at[pl.ds(src_row, 1)], dst.at[pl.ds(dst_row, 1)], sem)


def _dispatch_kernel(dst1_ref, dst2_ref, x_ref, zeros_hbm, xs_hbm, sem, *, tr):
    del zeros_hbm
    base = pl.program_id(0) * tr

    def each_row(fn):
        def body(r, _):
            t = base + r
            fn(_row_copy(x_ref, r, xs_hbm, dst1_ref[t], sem))
            fn(_row_copy(x_ref, r, xs_hbm, dst2_ref[t], sem))
            return 0

        lax.fori_loop(0, tr, body, 0, unroll=DMA_LOOP_UNROLL)

    each_row(lambda c: c.start())
    each_row(lambda c: c.wait())


def _dispatch(x, dst1, dst2, n_slots, tr):
    t, d = x.shape
    grid_spec = pltpu.PrefetchScalarGridSpec(
        num_scalar_prefetch=2,
        grid=(t // tr,),
        in_specs=[pl.BlockSpec((tr, d), lambda i, d1, d2: (i, 0)), pl.BlockSpec(memory_space=pl.ANY)],
        out_specs=pl.BlockSpec(memory_space=pl.ANY),
        scratch_shapes=[pltpu.SemaphoreType.DMA(())],
    )
    return pl.pallas_call(
        functools.partial(_dispatch_kernel, tr=tr),
        grid_spec=grid_spec,
        out_shape=jax.ShapeDtypeStruct((n_slots, d), F32),
        input_output_aliases={3: 0},
        compiler_params=_cparams("arbitrary"),
        name="moe_dispatch",
    )(dst1, dst2, x, jnp.zeros((n_slots, d), F32))


def _ffn_grouped_kernel(te_ref, tv_ref, x_ref, gpre_ref, wg_ref, wu_ref, wd_ref, y_ref, xn_scr):
    del te_ref
    i, f = pl.program_id(0), pl.program_id(1)
    valid = tv_ref[i] != 0

    @pl.when(f == 0)
    def _():
        xn_scr[...] = _rms(x_ref[...], gpre_ref[...]).astype(BF16)
        y_ref[...] = jnp.zeros_like(y_ref)

    @pl.when(valid)
    def _():
        xn = xn_scr[...]
        h = (jax.nn.silu(_dot(xn, wg_ref[...])) * _dot(xn, wu_ref[...])).astype(BF16)
        y_ref[...] += _dot(h, wd_ref[...])


def _ffn_grouped(xs, tile_expert, tile_valid, g_pre, w_gate, w_up, w_down, tm, tf):
    n_slots, d = xs.shape
    ff = w_gate.shape[2]
    row = pl.BlockSpec((tm, d), lambda i, f, te, tv: (i, 0))
    grid_spec = pltpu.PrefetchScalarGridSpec(
        num_scalar_prefetch=2,
        grid=(n_slots // tm, ff // tf),
        in_specs=[row,
                  pl.BlockSpec((1, d), lambda i, f, te, tv: (0, 0)),
                  pl.BlockSpec((None, d, tf), lambda i, f, te, tv: (te[i], 0, f)),
                  pl.BlockSpec((None, d, tf), lambda i, f, te, tv: (te[i], 0, f)),
                  pl.BlockSpec((None, tf, d), lambda i, f, te, tv: (te[i], f, 0))],
        out_specs=row,
        scratch_shapes=[pltpu.VMEM((tm, d), BF16)],
    )
    return pl.pallas_call(
        _ffn_grouped_kernel,
        grid_spec=grid_spec,
        out_shape=jax.ShapeDtypeStruct((n_slots, d), F32),
        compiler_params=_cparams("parallel", "arbitrary"),
        name="moe_ffn_grouped",
    )(tile_expert, tile_valid, xs, g_pre, w_gate, w_up, w_down)


def _combine_kernel(dst1_ref, dst2_ref, y_hbm, x_ref, meta_ref, gpost_ref, o_ref, ybuf, sems, *, tm, blk0):
    i, n = pl.program_id(0), pl.num_programs(0)

    def rows(step, slot, fn):
        base = (blk0 + step) * tm

        def body(r, _):
            t = base + r
            fn(_row_copy(y_hbm, dst1_ref[t], ybuf.at[slot, 0], r, sems.at[slot]))
            fn(_row_copy(y_hbm, dst2_ref[t], ybuf.at[slot, 1], r, sems.at[slot]))
            return 0

        lax.fori_loop(0, tm, body, 0, unroll=DMA_LOOP_UNROLL)

    @pl.when(i == 0)
    def _():
        rows(0, 0, lambda c: c.start())

    for slot in range(2):
        @pl.when((i + 1 < n) & ((i + 1) % 2 == slot))
        def _(slot=slot):
            rows(i + 1, slot, lambda c: c.start())

    for slot in range(2):
        @pl.when(i % 2 == slot)
        def _(slot=slot):
            rows(i, slot, lambda c: c.wait())
            meta = meta_ref[...]
            g1 = meta[:, META_G1:META_G1 + 1]
            g2 = meta[:, META_G2:META_G2 + 1]
            mix = g1 * ybuf[slot, 0] + g2 * ybuf[slot, 1]
            o_ref[...] = x_ref[...] + _rms(mix, gpost_ref[...])


def _combine(y, x, meta, dst1, dst2, g_post, tm, row0, n_rows):
    d = x.shape[1]
    blk0 = row0 // tm
    grid_spec = pltpu.PrefetchScalarGridSpec(
        num_scalar_prefetch=2,
        grid=(n_rows // tm,),
        in_specs=[pl.BlockSpec(memory_space=pl.ANY),
                  pl.BlockSpec((tm, d), lambda i, d1, d2: (blk0 + i, 0)),
                  pl.BlockSpec((tm, LANES), lambda i, d1, d2: (blk0 + i, 0)),
                  pl.BlockSpec((1, d), lambda i, d1, d2: (0, 0))],
        out_specs=pl.BlockSpec((tm, d), lambda i, d1, d2: (i, 0)),
        scratch_shapes=[pltpu.VMEM((2, 2, tm, d), F32), pltpu.SemaphoreType.DMA((2,))],
    )
    return pl.pallas_call(
        functools.partial(_combine_kernel, tm=tm, blk0=blk0),
        grid_spec=grid_spec,
        out_shape=jax.ShapeDtypeStruct((n_rows, d), F32),
        compiler_params=_cparams("arbitrary"),
        name="moe_combine",
    )(dst1, dst2, y, x, meta, g_post)


def _moe_plan(meta, counts, n_experts, tm):
    t = meta.shape[0]
    cnt = counts[0, :n_experts].astype(jnp.int32)
    padded = ((cnt + tm - 1) // tm) * tm
    ends = jnp.cumsum(padded)
    offs = ends - padded
    e1, e2 = meta[:, META_E1].astype(jnp.int32), meta[:, META_E2].astype(jnp.int32)
    dst1 = offs[e1] + meta[:, META_R1].astype(jnp.int32)
    dst2 = offs[e2] + meta[:, META_R2].astype(jnp.int32)
    n_tiles = (2 * t + n_experts * (tm - 1)) // tm
    starts = jnp.arange(n_tiles, dtype=jnp.int32) * tm
    tile_expert = jnp.minimum(jnp.sum(starts[:, None] >= ends[None, :], axis=1), n_experts - 1).astype(jnp.int32)
    tile_valid = (starts < ends[-1]).astype(jnp.int32)
    return dst1, dst2, tile_expert, tile_valid, n_tiles * tm


def _log_sigmoid(z):
    return jnp.minimum(z, 0.0) - jnp.log(1.0 + jnp.exp(-jnp.abs(z)))


def _fox_proj_prompt_kernel(x_ref, gpre_ref, wq_ref, wkvt_ref, wft_ref, bf_ref,
                            q_ref, kt_ref, vt_ref, lft_ref, *, scale):
    xn = _rms(x_ref[...], gpre_ref[...]).astype(BF16)
    q_ref[...] = (_dot(xn, wq_ref[...]) * scale).astype(BF16)
    d = kt_ref.shape[0]
    kvt = _dot_nt(wkvt_ref[...], xn)
    kt_ref[...] = kvt[:d]
    vt_ref[...] = kvt[d:]
    lft_ref[...] = _log_sigmoid(_dot_nt(wft_ref[...], xn) + bf_ref[...])


def _fox_proj_prompt(x, g_pre, wq, wkvt, wft, b_f, scale, tm):
    nb, nl, d = x.shape
    nh = wft.shape[0]
    const = lambda b, i: (0, 0)
    return pl.pallas_call(
        functools.partial(_fox_proj_prompt_kernel, scale=scale),
        grid=(nb, nl // tm),
        in_specs=[pl.BlockSpec((None, tm, d), lambda b, i: (b, i, 0)),
                  pl.BlockSpec((1, d), const),
                  pl.BlockSpec(wq.shape, const),
                  pl.BlockSpec(wkvt.shape, const),
                  pl.BlockSpec(wft.shape, const),
                  pl.BlockSpec((nh, 1), const)],
        out_specs=[pl.BlockSpec((None, tm, d), lambda b, i: (b, i, 0)),
                   pl.BlockSpec((None, d, tm), lambda b, i: (b, 0, i)),
                   pl.BlockSpec((None, d, tm), lambda b, i: (b, 0, i)),
                   pl.BlockSpec((None, nh, tm), lambda b, i: (b, 0, i))],
        out_shape=[jax.ShapeDtypeStruct((nb, nl, d), BF16),
                   jax.ShapeDtypeStruct((nb, d, nl), F32),
                   jax.ShapeDtypeStruct((nb, d, nl), F32),
                   jax.ShapeDtypeStruct((nb, nh, nl), F32)],
        compiler_params=_cparams("parallel", "parallel"),
        name="fox_proj_prompt",
    )(x, g_pre, wq, wkvt, wft, b_f)


def _tri01(n, rel):
    r = lax.broadcasted_iota(jnp.int32, (n, n), 0)
    c = lax.broadcasted_iota(jnp.int32, (n, n), 1)
    return rel(r, c).astype(BF16)


def _cumsum_lanes_kernel(x_ref, o_ref):
    upper = _tri01(LANES, lambda r, c: r <= c)
    carry = jnp.zeros((x_ref.shape[0], 1), F32)
    for blk in range(x_ref.shape[1] // LANES):
        sl = slice(blk * LANES, (blk + 1) * LANES)
        cum = _dot_exact_rhs01(x_ref[:, sl], upper) + carry
        o_ref[:, sl] = cum
        carry = cum[:, LANES - 1:LANES]


def _cumsum_lanes(x):
    nb, r, nl = x.shape
    spec = pl.BlockSpec((None, r, nl), lambda b: (b, 0, 0))
    return pl.pallas_call(
        _cumsum_lanes_kernel, grid=(nb,), in_specs=[spec], out_specs=spec,
        out_shape=jax.ShapeDtypeStruct(x.shape, F32),
        compiler_params=_cparams("parallel"), name="logf_cumsum",
    )(x)


def _prompt_attn_part(q_ref, kt_ref, vt_ref, c_ref, o_ref, m_scr, l_scr, acc_scr, hpair, qi, phase,
                      *, tq, tk, hd):
    heads = LANES // hd
    q0 = pl.multiple_of(qi * tq, tq)
    n_full = q0 // tk
    row = lax.broadcasted_iota(jnp.int32, (tq, tk), 0)
    col = lax.broadcasted_iota(jnp.int32, (tq, tk), 1)
    hslice = [slice(hh * hd, (hh + 1) * hd) for hh in range(heads)]
    qs = [q_ref[pl.ds(q0, tq), hs] for hs in hslice]

    def tile(k0, carry, masked):
        out = []
        for hh, hs in enumerate(hslice):
            m, l, acc = carry[hh]
            kt = kt_ref[hs, pl.ds(k0, tk)].astype(BF16)
            vt = vt_ref[hs, pl.ds(k0, tk)].astype(BF16)
            cc = c_ref[pl.ds(hpair * heads + hh, 1), pl.ds(k0, tk)]
            s = _dot(qs[hh], kt) - cc
            if masked:
                s = jnp.where(q0 + row >= k0 + col, s, NEG_INF)
            blocks = [s[:, j * LANES:(j + 1) * LANES] for j in range(tk // LANES)]
            m_new = jnp.maximum(m, jnp.max(functools.reduce(jnp.maximum, blocks), axis=-1, keepdims=True))
            alpha = jnp.exp(m - m_new)
            p_blocks = [jnp.exp(blk - m_new) for blk in blocks]
            l = alpha * l + jnp.sum(functools.reduce(jnp.add, p_blocks), axis=-1, keepdims=True)
            p = jnp.concatenate([pb.astype(BF16) for pb in p_blocks], axis=1)
            acc = alpha[:, :hd] * acc + _dot_nt(p, vt)
            out.append((m_new, l, acc))
        return tuple(out)

    n_first = (n_full + 1) // 2
    first = phase == 0
    init = tuple((jnp.where(first, NEG_INF, m_scr[hh]), jnp.where(first, 0.0, l_scr[hh]),
                  jnp.where(first, 0.0, acc_scr[hh])) for hh in range(heads))
    lo = jnp.where(first, 0, n_first)
    carry = lax.fori_loop(0, jnp.where(first, n_first, n_full - n_first),
                          lambda j, cr: tile(pl.multiple_of((lo + j) * tk, tk), cr, False), init)

    @pl.when(first)
    def _():
        for hh in range(heads):
            m_scr[hh], l_scr[hh], acc_scr[hh] = carry[hh]

    @pl.when(phase == 1)
    def _():
        done = tile(pl.multiple_of(n_full * tk, tk), carry, True)
        for hh, hs in enumerate(hslice):
            _, l, acc = done[hh]
            o_ref[:, hs] = (acc / l[:, :hd]).astype(o_ref.dtype)


def _fox_proj_sample_kernel(x_ref, gpre_ref, w_ref, wf_ref, bf_ref, q_ref, k_ref, v_ref, lf_ref, *, scale):
    xn = _rms(x_ref[...], gpre_ref[...]).astype(BF16)
    d = q_ref.shape[1]
    proj = _dot(xn, w_ref[...])
    q_ref[...] = proj[:, :d] * scale
    k_ref[...] = proj[:, d:2 * d]
    v_ref[...] = proj[:, 2 * d:]
    lf_ref[...] = _log_sigmoid(_dot(xn, wf_ref[...]) + bf_ref[...])


def _fox_proj_sample(x, g_pre, w_qkv, w_f, b_f, scale):
    t, d = x.shape
    nh = w_f.shape[1]
    wf = jnp.zeros((d, LANES), BF16).at[:, :nh].set(w_f)
    bf = jnp.zeros((1, LANES), F32).at[0, :nh].set(b_f)
    full = lambda a: pl.BlockSpec(a.shape, lambda i: (0,) * a.ndim)
    row = pl.BlockSpec((t, d), lambda i: (0, 0))
    return pl.pallas_call(
        functools.partial(_fox_proj_sample_kernel, scale=scale),
        grid=(1,),
        in_specs=[row, full(g_pre), full(w_qkv), full(wf), full(bf)],
        out_specs=[row, row, row, pl.BlockSpec((t, LANES), lambda i: (0, 0))],
        out_shape=[jax.ShapeDtypeStruct((t, d), F32)] * 3 + [jax.ShapeDtypeStruct((t, LANES), F32)],
        compiler_params=_cparams("arbitrary"),
        name="fox_proj_sample",
    )(x, g_pre, w_qkv, wf, bf)


def _paged_attn_part(q_ref, kc_refs, vc_refs, lfc_refs, kn_ref, vn_ref, lfn_ref, o_ref,
                     qbd, m_scr, l_scr, acc_scr, carry, bias_scr, p, n_chunks, *, ns, hd):
    gp = len(kc_refs)
    rows, d = qbd.shape
    nh = d // hd
    r_i = lax.broadcasted_iota(jnp.int32, (rows, d), 0)
    c_i = lax.broadcasted_iota(jnp.int32, (rows, d), 1)
    bmask = (r_i // ns) == (c_i // hd)
    expand = (lax.broadcasted_iota(jnp.int32, (rows, nh), 0) // ns
              == lax.broadcasted_iota(jnp.int32, (rows, nh), 1)).astype(BF16)

    @pl.when(p == 0)
    def _():
        qt = jnp.concatenate([q_ref[...]] * nh, axis=0)
        qbd[...] = jnp.where(bmask, qt, 0.0).astype(BF16)
        m_scr[...] = jnp.full_like(m_scr, NEG_INF)
        l_scr[...] = jnp.zeros_like(l_scr)
        acc_scr[...] = jnp.zeros_like(acc_scr)
        carry[...] = jnp.zeros_like(carry)

    def update(s, v_nt=None, v_nn=None):
        m_new = jnp.maximum(m_scr[...], jnp.max(s, axis=-1, keepdims=True))
        alpha = jnp.exp(m_scr[...] - m_new)
        pr = jnp.exp(s - m_new)
        l_scr[...] = alpha * l_scr[...] + jnp.sum(pr, axis=-1, keepdims=True)
        pv = _dot_nt(pr.astype(BF16), v_nt) if v_nt is not None else _dot(pr.astype(BF16), v_nn)
        acc_scr[...] = alpha * acc_scr[...] + pv
        m_scr[...] = m_new

    lfs = [r[...] for r in lfc_refs]
    after = _tri01(LANES, lambda r, c: r > c)
    within = _dot_exact_rhs01(jnp.concatenate(lfs, axis=0), after)
    run = carry[...]
    for k in reversed(range(gp)):
        w_k = within[k * nh:(k + 1) * nh]
        bias_scr[:, k * LANES:(k + 1) * LANES] = run + w_k
        run = run + (w_k[:, 0:1] + lfs[k][:, 0:1])
    carry[...] = run
    kcat = jnp.concatenate([r[...].astype(BF16) for r in kc_refs], axis=1)
    vcat = jnp.concatenate([r[...].astype(BF16) for r in vc_refs], axis=1)
    s = _dot(qbd[...], kcat)
    s = jnp.concatenate([s[h * ns:(h + 1) * ns] + bias_scr[h:h + 1, :] for h in range(nh)], axis=0)
    update(s, v_nt=vcat)

    @pl.when(p == n_chunks - 1)
    def _():
        upto = _tri01(LANES, lambda r, c: r <= c)
        cn = _dot_exact_rhs01(lfn_ref[...], upto)
        pad = jnp.zeros((LANES - ns, d), F32)
        kn = jnp.concatenate([kn_ref[...], pad], axis=0).astype(BF16)
        vn = jnp.concatenate([vn_ref[...], pad], axis=0).astype(BF16)
        sn = _dot_nt(qbd[...], kn) - _dot_exact_lhs01(expand, cn)
        qpos = lax.broadcasted_iota(jnp.int32, (rows, LANES), 0) % ns
        kpos = lax.broadcasted_iota(jnp.int32, (rows, LANES), 1)
        update(jnp.where(kpos <= qpos, sn, NEG_INF), v_nn=vn)
        o = jnp.where(bmask, acc_scr[...] / l_scr[...], 0.0)
        out = o[0:ns]
        for h in range(1, nh):
            out = out + o[h * ns:(h + 1) * ns]
        o_ref[...] = out


def _fox_attn_kernel(tbl_ref, *refs, gp, n_chunks, n_sample_steps, n_hpairs, n_qtiles, n_prompt_steps,
                     ns, hd, tq, tk):
    del tbl_ref
    sq_ref, refs = refs[0], refs[1:]
    kc_refs, vc_refs, lfc_refs = refs[:gp], refs[gp:2 * gp], refs[2 * gp:3 * gp]
    kn_ref, vn_ref, lfn_ref, q_ref, kt_ref, vt_ref, c_ref, os_ref, op_ref = refs[3 * gp:3 * gp + 9]
    decode_scratch, prompt_scratch = refs[3 * gp + 9:-3], refs[-3:]
    s = pl.program_id(0)

    @pl.when(s == 0)
    def _():
        for ref in prompt_scratch:
            ref[...] = jnp.zeros_like(ref)

    @pl.when(s < n_sample_steps)
    def _():
        _paged_attn_part(sq_ref, kc_refs, vc_refs, lfc_refs, kn_ref, vn_ref, lfn_ref, os_ref, *decode_scratch,
                         s % n_chunks, n_chunks, ns=ns, hd=hd)

    @pl.when(s < n_prompt_steps)
    def _():
        unit = s // 2
        _prompt_attn_part(q_ref, kt_ref, vt_ref, c_ref, op_ref, *prompt_scratch,
                          (unit // n_qtiles) % n_hpairs, unit % n_qtiles, s % 2, tq=tq, tk=tk, hd=hd)


def _fox_attn(page_table, sq, kc, vc, lfc, kn, vn, lfn, q, kt, vt, c, hd):
    db, ns, d = sq.shape
    n_pages = page_table.shape[1]
    nh = d // hd
    rows = nh * ns
    page = kc.shape[2]
    assert rows == LANES and page == LANES
    gp = next(g for g in (8, 4, 2, 1) if n_pages % g == 0)
    n_chunks = n_pages // gp
    n_sample_steps = db * n_chunks

    nb, nl, _ = q.shape
    tq = tk = min(nl, 512)
    n_hpairs, n_qtiles = d // LANES, nl // tq
    n_prompt_steps = nb * n_hpairs * n_qtiles * 2

    def sample_pos(s):
        s = jnp.minimum(s, n_sample_steps - 1)
        return s // n_chunks, s % n_chunks

    def prompt_pos(s):
        s = jnp.minimum(s, n_prompt_steps - 1)
        unit = s // 2
        return unit // (n_hpairs * n_qtiles), (unit // n_qtiles) % n_hpairs, unit % n_qtiles

    seq = pl.BlockSpec((None, ns, d), lambda s, tbl: (sample_pos(s)[0], 0, 0))

    def pg(k):
        def index(s, tbl):
            b, p = sample_pos(s)
            return tbl[b, (n_chunks - 1 - p) * gp + k], 0, 0
        return index

    def pidx(fn):
        return lambda s, tbl: fn(*prompt_pos(s))

    grid_spec = pltpu.PrefetchScalarGridSpec(
        num_scalar_prefetch=1,
        grid=(max(n_sample_steps, n_prompt_steps),),
        in_specs=([seq]
                  + [pl.BlockSpec((None, d, page), pg(k)) for k in range(gp)]
                  + [pl.BlockSpec((None, d, page), pg(k)) for k in range(gp)]
                  + [pl.BlockSpec((None, nh, page), pg(k)) for k in range(gp)]
                  + [seq, seq, pl.BlockSpec((None, nh, LANES), lambda s, tbl: (sample_pos(s)[0], 0, 0))]
                  + [pl.BlockSpec((None, nl, LANES), pidx(lambda b, h, i: (b, 0, h))),
                     pl.BlockSpec((None, LANES, nl), pidx(lambda b, h, i: (b, h, 0))),
                     pl.BlockSpec((None, LANES, nl), pidx(lambda b, h, i: (b, h, 0))),
                     pl.BlockSpec((None, nh, nl), pidx(lambda b, h, i: (b, 0, 0)))]),
        out_specs=[seq, pl.BlockSpec((None, tq, LANES), pidx(lambda b, h, i: (b, i, h)))],
        scratch_shapes=[pltpu.VMEM((rows, d), BF16),
                        pltpu.VMEM((rows, 1), F32),
                        pltpu.VMEM((rows, 1), F32),
                        pltpu.VMEM((rows, d), F32),
                        pltpu.VMEM((nh, 1), F32),
                        pltpu.VMEM((nh, gp * page), F32),
                        pltpu.VMEM((LANES // hd, tq, LANES), F32),
                        pltpu.VMEM((LANES // hd, tq, LANES), F32),
                        pltpu.VMEM((LANES // hd, tq, hd), F32)],
    )
    return pl.pallas_call(
        functools.partial(_fox_attn_kernel, gp=gp, n_chunks=n_chunks, n_sample_steps=n_sample_steps,
                          n_hpairs=n_hpairs, n_qtiles=n_qtiles, n_prompt_steps=n_prompt_steps,
                          ns=ns, hd=hd, tq=tq, tk=tk),
        grid_spec=grid_spec,
        out_shape=[jax.ShapeDtypeStruct((db, ns, d), F32), jax.ShapeDtypeStruct((nb, nl, d), BF16)],
        compiler_params=_cparams("arbitrary"),
        name="fox_attn",
    )(page_table, sq, *([kc] * gp), *([vc] * gp), *([lfc] * gp), kn, vn, lfn, q, kt, vt, c)


def _div_tile(t, candidates):
    return next((tm for tm in candidates if t % tm == 0), t)


def _row_tile(t):
    return _div_tile(t, (1024, 512, 256, 128))


def kernel(x_prompt, x_sample, state_ssm_re, state_ssm_im, cache_k, cache_v, cache_logf, page_table, norm_mix_pre, norm_mix_post, norm_ffn_pre, norm_ffn_post, ssm_w_in, ssm_lambda_re, ssm_lambda_im, ssm_log_dt, ssm_b_re, ssm_b_im, ssm_c_re, ssm_c_im, ssm_d, ssm_w_glu, ssm_b_glu, ssm_w_out, fox_w_in, fox_b_f, fox_w_out, ffn_w_gate, ffn_w_up, ffn_w_down, moe_w_router, moe_b_router, moe_w_gate, moe_w_up, moe_w_down):
    nb, nl, d = x_prompt.shape
    db, ns, _ = x_sample.shape
    n_groups, n_state = ssm_lambda_re.shape[1:]
    nh = fox_b_f.shape[1]
    hd = d // nh
    scale = hd ** -0.5
    ng = d // LANES
    streams = [x_prompt.reshape(nb * nl, d), x_sample.reshape(db * ns, d)]
    dims = [(nb, nl), (db, ns)]
    tms = [_row_tile(x.shape[0]) for x in streams]
    vec = lambda a: a.reshape(1, -1)

    li = 0
    w_in = ssm_w_in[li].astype(BF16)
    w_glu = ssm_w_glu[li].astype(BF16)
    w_out = ssm_w_out[li].astype(BF16)
    s5w = _s5_prepare(ssm_lambda_re[li], ssm_lambda_im[li], ssm_log_dt[li], ssm_b_re[li], ssm_b_im[li],
                      ssm_c_re[li], ssm_c_im[li])
    wg, wu, wd = (w[li].astype(BF16) for w in (ffn_w_gate, ffn_w_up, ffn_w_down))
    h0s = [jnp.zeros((ng, nb, 2 * (n_groups // ng) * n_state), F32),
           _s5_state_in(state_ssm_re[li], state_ssm_im[li], ng)]
    ssm_states = []
    for si in range(2):
        x, (b_, l_), tm = streams[si], dims[si], tms[si]
        u = _norm_matmul(x, vec(norm_mix_pre[0]), w_in, tm)
        g_act, h_fin = _s5_core(u.reshape(b_, l_, d), h0s[si], *s5w, vec(ssm_d[li]))
        ssm_states.append(_s5_state_out(h_fin, n_groups, n_state))
        x = _glu_out(g_act.reshape(b_ * l_, d), x, w_glu, vec(ssm_b_glu[li]), w_out, vec(norm_mix_post[0]), tm)
        streams[si] = _ffn(x, vec(norm_ffn_pre[0]), wg, wu, wd, vec(norm_ffn_post[0]), tm, 512)

    w_fox = fox_w_in[li]
    w_fox_t = w_fox.T
    wq = w_fox[:, :d].astype(BF16)
    wkvt = w_fox_t[d:3 * d].astype(BF16)
    wft = w_fox_t[3 * d:].astype(BF16)
    w_o = fox_w_out[li].astype(BF16)

    xp = streams[0]
    q, kt, vt, lft = _fox_proj_prompt(xp.reshape(nb, nl, d), vec(norm_mix_pre[1]), wq, wkvt, wft,
                                      fox_b_f[li].reshape(nh, 1), scale, min(nl, 1024))
    c = _cumsum_lanes(lft)
    tp, tsm = nb * nl, db * ns
    k_prompt = kt.reshape(nb, nh, hd, nl).transpose(0, 3, 1, 2)[None]
    v_prompt = vt.reshape(nb, nh, hd, nl).transpose(0, 3, 1, 2)[None]
    logf_prompt = lft.transpose(0, 2, 1)[None]

    xs = streams[1]
    qs, ks, vs, lfs = _fox_proj_sample(xs, vec(norm_mix_pre[1]), w_fox[:, :3 * d].astype(BF16),
                                       w_fox[:, 3 * d:].astype(BF16), fox_b_f[li], scale)
    n_pool, page = cache_k.shape[1], cache_k.shape[2]
    kc = cache_k[li].transpose(0, 2, 3, 1).reshape(n_pool, d, page)
    vc = cache_v[li].transpose(0, 2, 3, 1).reshape(n_pool, d, page)
    lfc = cache_logf[li].transpose(0, 2, 1)
    lfs = lfs[:, :nh]
    lfn = jnp.zeros((db, nh, LANES), F32).at[:, :, :ns].set(lfs.reshape(db, ns, nh).transpose(0, 2, 1))
    os_, o = _fox_attn(page_table, qs.reshape(db, ns, d), kc, vc, lfc,
                       ks.reshape(db, ns, d), vs.reshape(db, ns, d), lfn, q, kt, vt, c, hd)
    small = (256, 128, 64, 32, 16, 8)
    x_all = _matmul_post_merged(o.reshape(tp, d), xp, os_.reshape(tsm, d), xs, w_o, vec(norm_mix_post[1]),
                                _div_tile(tsm, small))
    k_sample = ks.reshape(1, db, ns, nh, hd)
    v_sample = vs.reshape(1, db, ns, nh, hd)
    logf_sample = lfs.reshape(1, db, ns, nh)

    mg, mu, md = (w[li].astype(BF16) for w in (moe_w_gate, moe_w_up, moe_w_down))
    n_experts = mg.shape[0]
    t_route = _div_tile(tp + tsm, small)
    meta, counts = _router(x_all, vec(norm_ffn_pre[1]), moe_w_router[li], moe_b_router[li], t_route)
    dst1, dst2, tile_expert, tile_valid, n_slots = _moe_plan(meta, counts, n_experts, MOE_ROW_TILE)
    x_sorted = _dispatch(x_all, dst1, dst2, n_slots, t_route)
    y_sorted = _ffn_grouped(x_sorted, tile_expert, tile_valid, vec(norm_ffn_pre[1]), mg, mu, md, MOE_ROW_TILE, 512)
    outs = [_combine(y_sorted, x_all, meta, dst1, dst2, vec(norm_ffn_post[1]), _div_tile(n, small), r0, n)
            for r0, n in ((0, tp), (tp, tsm))]

    (re_p, im_p), (re_s, im_s) = ssm_states
    return (outs[0].reshape(nb, nl, d), outs[1].reshape(db, ns, d),
            re_p[None], im_p[None], re_s[None], im_s[None],
            k_prompt, v_prompt, logf_prompt, k_sample, v_sample, logf_sample)
```

```python
import functools

import jax
import jax.numpy as jnp
from jax import lax
from jax.experimental import pallas as pl
from jax.experimental.pallas import tpu as pltpu

F32 = jnp.float32
BF16 = jnp.bfloat16
RMS_EPS = 1e-6
LANES = 128
SUBLANES = 8
V7X_MXU_DIM = 256
V7X_VMEM_LIMIT_BYTES = 56 << 20
S5_CHUNK = 8
S5_SCAN_UNROLL = 8
FFN_ROW_TILE = 512
DMA_LOOP_UNROLL = 8
NEG_INF = float("-inf")


def _cparams(*sem):
    return pltpu.CompilerParams(dimension_semantics=sem, vmem_limit_bytes=V7X_VMEM_LIMIT_BYTES)


def _rms(x, g):
    return x * lax.rsqrt(jnp.mean(x * x, axis=-1, keepdims=True) + RMS_EPS) * g


def _dot(a, b):
    return jnp.dot(a, b, preferred_element_type=F32)


def _dot_nt(a, b):
    return lax.dot_general(a, b, (((1,), (1,)), ((), ())), preferred_element_type=F32)


def _split3(x):
    hi = x.astype(BF16)
    r = x - hi.astype(F32)
    mid = r.astype(BF16)
    lo = (r - mid.astype(F32)).astype(BF16)
    return hi, mid, lo


def _dot_exact_rhs01(x, m01):
    hi, mid, lo = _split3(x)
    return _dot(hi, m01) + _dot(mid, m01) + _dot(lo, m01)


def _dot_exact_lhs01(m01, x):
    hi, mid, lo = _split3(x)
    return _dot(m01, hi) + _dot(m01, mid) + _dot(m01, lo)


def _norm_matmul_kernel(x_ref, g_ref, w_ref, o_ref):
    xn = _rms(x_ref[...], g_ref[...]).astype(BF16)
    o_ref[...] = _dot(xn, w_ref[...])


def _norm_matmul(x, g, w, tm):
    t, d = x.shape
    n = w.shape[1]
    return pl.pallas_call(
        _norm_matmul_kernel,
        grid=(t // tm,),
        in_specs=[pl.BlockSpec((tm, d), lambda i: (i, 0)),
                  pl.BlockSpec((1, d), lambda i: (0, 0)),
                  pl.BlockSpec((d, n), lambda i: (0, 0))],
        out_specs=pl.BlockSpec((tm, n), lambda i: (i, 0)),
        out_shape=jax.ShapeDtypeStruct((t, n), F32),
        compiler_params=_cparams("parallel"),
        name="norm_matmul",
    )(x, g, w)


def _s5_prepare(lam_re, lam_im, log_dt, b_re, b_im, c_re, c_im):
    tc = S5_CHUNK
    g, p = lam_re.shape
    c = b_re.shape[-1]
    ng = (g * c) // LANES
    gl = g // ng
    hp = lax.Precision.HIGHEST
    dt = jnp.exp(log_dt)[:, None]
    mag = jnp.exp(lam_re * dt)
    ar = mag * jnp.cos(lam_im * dt)
    ai = mag * jnp.sin(lam_im * dt)
    den = lam_re * lam_re + lam_im * lam_im
    nr = ar - 1.0
    coef_r = (nr * lam_re + ai * lam_im) / den
    coef_i = (ai * lam_re - nr * lam_im) / den
    bbr = coef_r[..., None] * b_re - coef_i[..., None] * b_im
    bbi = coef_r[..., None] * b_im + coef_i[..., None] * b_re
    prs, pis = [jnp.ones_like(ar)], [jnp.zeros_like(ar)]
    for _ in range(tc):
        nr_, ni_ = prs[-1] * ar - pis[-1] * ai, prs[-1] * ai + pis[-1] * ar
        prs.append(nr_)
        pis.append(ni_)
    pr = jnp.stack(prs)
    pi = jnp.stack(pis)
    prs_ = pr[tc - 1 - jnp.arange(tc)][..., None]
    pis_ = pi[tc - 1 - jnp.arange(tc)][..., None]
    ms = jnp.stack([prs_ * bbr - pis_ * bbi, prs_ * bbi + pis_ * bbr])
    ms = ms.transpose(2, 1, 4, 0, 3).reshape(ng, gl, tc, c, 2 * p)
    m_state = ms.transpose(0, 2, 1, 3, 4).reshape(ng, tc * gl * c, 2 * p)

    er = pr[:tc, :, :, None] * bbr - pi[:tc, :, :, None] * bbi
    ei = pr[:tc, :, :, None] * bbi + pi[:tc, :, :, None] * bbr
    kk = (jnp.einsum("gdp,kgpc->kgdc", c_re, er, precision=hp)
          - jnp.einsum("gdp,kgpc->kgdc", c_im, ei, precision=hp))
    lag = jnp.arange(tc)[None, :] - jnp.arange(tc)[:, None]
    kt = jnp.where((lag >= 0)[:, :, None, None, None], kk[jnp.clip(lag, 0)], 0.0)
    kt = kt.transpose(2, 0, 4, 1, 3).reshape(ng, gl, tc, c, tc * c)
    m_intra = kt.transpose(0, 2, 1, 3, 4).reshape(ng, tc * gl * c, tc * c)

    pr1 = pr[1:].transpose(1, 2, 0)[:, :, :, None]
    pi1 = pi[1:].transpose(1, 2, 0)[:, :, :, None]
    cre = c_re.transpose(0, 2, 1)[:, :, None, :]
    cim = c_im.transpose(0, 2, 1)[:, :, None, :]
    hh = jnp.stack([cre * pr1 - cim * pi1, -(cre * pi1 + cim * pr1)])
    m_carry = hh.reshape(2, ng, gl * p, tc * c).transpose(1, 0, 2, 3).reshape(ng, 2 * gl * p, tc * c)

    w_state = _s5_expand(m_state, gl, row_group=c, col_block=gl * p, col_sub=p)
    w_intra = _s5_expand(m_intra, gl, row_group=c, col_block=gl * c, col_sub=c)
    w_carry = _s5_expand(m_carry, gl, row_group=p, col_block=gl * c, col_sub=c)
    a_pow = jnp.concatenate([pr[tc].reshape(ng, 1, gl * p), pi[tc].reshape(ng, 1, gl * p)], axis=-1)
    return w_state, w_intra, w_carry, a_pow


def _s5_expand_kernel(m_ref, o_ref, *, gl, row_group, col_block, col_sub):
    rows, cols = o_ref.shape
    s_i = lax.broadcasted_iota(jnp.int32, (m_ref.shape[1], cols), 0)
    j_i = lax.broadcasted_iota(jnp.int32, (m_ref.shape[1], cols), 1)
    place = (s_i == (j_i // col_block) * col_sub + j_i % col_sub).astype(BF16)
    spread = _dot(m_ref[...].astype(BF16), place)
    r_g = (lax.broadcasted_iota(jnp.int32, (rows, cols), 0) // row_group) % gl
    c_g = (lax.broadcasted_iota(jnp.int32, (rows, cols), 1) % col_block) // col_sub
    o_ref[...] = jnp.where(r_g == c_g, spread, 0.0).astype(o_ref.dtype)


def _s5_expand(m, gl, row_group, col_block, col_sub):
    ng, rows, k = m.shape
    cols = k * gl
    return pl.pallas_call(
        functools.partial(_s5_expand_kernel, gl=gl, row_group=row_group, col_block=col_block, col_sub=col_sub),
        grid=(ng,),
        in_specs=[pl.BlockSpec((None, rows, k), lambda a: (a, 0, 0))],
        out_specs=pl.BlockSpec((None, rows, cols), lambda a: (a, 0, 0)),
        out_shape=jax.ShapeDtypeStruct((ng, rows, cols), BF16),
        compiler_params=_cparams("parallel"),
        name="s5_expand",
    )(m)


def _s5_core_kernel(u_ref, h0_ref, ws_ref, wi_ref, wh_ref, ap_ref, d_ref, g_ref, hout_ref,
                    s_scr, hs_scr, hcar, *, nb, nj):
    tc = S5_CHUNK
    rows = nb * nj
    nk = s_scr.shape[0]
    half = nk // 2
    lane_blk = lambda k: slice(k * LANES, (k + 1) * LANES)

    @pl.when(pl.program_id(1) == 0)
    def _():
        hcar[...] = h0_ref[...]

    def u_at(t):
        if nj == 1:
            return u_ref[:, t, :]
        return u_ref[:, pl.ds(t, nj, stride=tc), :].reshape(rows, LANES)

    us = [u_at(t) for t in range(tc)]
    x = jnp.concatenate([v.astype(BF16) for v in us], axis=1)
    s_all = _dot(x, ws_ref[...])
    for k in range(nk):
        s_scr[k] = s_all[:, lane_blk(k)]

    a_pow = [ap_ref[:, lane_blk(k)] for k in range(nk)]

    def rows_at(j):
        return slice(None) if nj == 1 else pl.ds(j, nb, stride=nj)

    def step(j, h):
        for k in range(nk):
            hs_scr[k, rows_at(j), :] = h[k]
        new_re, new_im = [], []
        for k in range(half):
            hr, hi, ar, ai = h[k], h[k + half], a_pow[k], a_pow[k + half]
            new_re.append(ar * hr - ai * hi + s_scr[k, rows_at(j), :])
            new_im.append(ar * hi + ai * hr + s_scr[k + half, rows_at(j), :])
        return tuple(new_re + new_im)

    h = tuple(hcar[:, lane_blk(k)] for k in range(nk))
    h = step(0, h) if nj == 1 else lax.fori_loop(0, nj, step, h, unroll=S5_SCAN_UNROLL)
    for k in range(nk):
        hcar[:, lane_blk(k)] = h[k]
        hout_ref[:, lane_blk(k)] = h[k]

    hs = jnp.concatenate([hs_scr[k].astype(BF16) for k in range(nk)], axis=1)
    y = _dot(x, wi_ref[...]) + _dot(hs, wh_ref[...])
    d = d_ref[...]
    for t in range(tc):
        gt = jax.nn.gelu(y[:, lane_blk(t)] + d * us[t])
        if nj == 1:
            g_ref[:, t, :] = gt
        else:
            g_ref[:, pl.ds(t, nj, stride=tc), :] = gt.reshape(nb, nj, LANES)


def _s5_core(u, h0, w_state, w_intra, w_carry, a_pow, d_skip):
    nb, nl, d = u.shape
    ng = d // LANES
    ts = min(nl, 512)
    nj = ts // S5_CHUNK
    sw = w_state.shape[2]
    kern = functools.partial(_s5_core_kernel, nb=nb, nj=nj)
    return pl.pallas_call(
        kern,
        grid=(ng, nl // ts),
        in_specs=[pl.BlockSpec((nb, ts, LANES), lambda g, s: (0, s, g)),
                  pl.BlockSpec((None, nb, sw), lambda g, s: (g, 0, 0)),
                  pl.BlockSpec((None,) + w_state.shape[1:], lambda g, s: (g, 0, 0)),
                  pl.BlockSpec((None,) + w_intra.shape[1:], lambda g, s: (g, 0, 0)),
                  pl.BlockSpec((None,) + w_carry.shape[1:], lambda g, s: (g, 0, 0)),
                  pl.BlockSpec((None, 1, sw), lambda g, s: (g, 0, 0)),
                  pl.BlockSpec((1, LANES), lambda g, s: (0, g))],
        out_specs=[pl.BlockSpec((nb, ts, LANES), lambda g, s: (0, s, g)),
                   pl.BlockSpec((None, nb, sw), lambda g, s: (g, 0, 0))],
        out_shape=[jax.ShapeDtypeStruct((nb, nl, d), F32),
                   jax.ShapeDtypeStruct((ng, nb, sw), F32)],
        scratch_shapes=[pltpu.VMEM((sw // LANES, nb * nj, LANES), F32),
                        pltpu.VMEM((sw // LANES, nb * nj, LANES), F32),
                        pltpu.VMEM((nb, sw), F32)],
        compiler_params=_cparams("parallel", "arbitrary"),
        name="s5_core",
    )(u, h0, w_state, w_intra, w_carry, a_pow, d_skip)


def _s5_state_in(re, im, ng):
    nb = re.shape[0]
    h = jnp.concatenate([re.reshape(nb, ng, -1), im.reshape(nb, ng, -1)], axis=-1)
    return h.transpose(1, 0, 2)


def _s5_state_out(h, g, p):
    ng, nb, sw = h.shape
    h = h.transpose(1, 0, 2)
    return h[..., :sw // 2].reshape(nb, g, p), h[..., sw // 2:].reshape(nb, g, p)


def _glu_out_kernel(g_ref, x_ref, wglu_ref, bglu_ref, wout_ref, gpost_ref, o_ref):
    g = g_ref[...]
    t = _dot(g.astype(BF16), wglu_ref[...]) + bglu_ref[...]
    z = (g * jax.nn.sigmoid(t)).astype(BF16)
    m = _dot(z, wout_ref[...])
    o_ref[...] = x_ref[...] + _rms(m, gpost_ref[...])


def _glu_out(g, x, w_glu, b_glu, w_out, g_post, tm):
    t, d = x.shape
    row = pl.BlockSpec((tm, d), lambda i: (i, 0))
    vec = pl.BlockSpec((1, d), lambda i: (0, 0))
    mat = pl.BlockSpec((d, d), lambda i: (0, 0))
    return pl.pallas_call(
        _glu_out_kernel,
        grid=(t // tm,),
        in_specs=[row, row, mat, vec, mat, vec],
        out_specs=row,
        out_shape=jax.ShapeDtypeStruct((t, d), F32),
        compiler_params=_cparams("parallel"),
        name="glu_out",
    )(g, x, w_glu, b_glu, w_out, g_post)


def _matmul_post_kernel(a1_ref, x1_ref, a2_ref, x2_ref, w_ref, gpost_ref, o_ref, *, n1):
    def emit(a_ref, x_ref):
        m = _dot(a_ref[...].astype(BF16), w_ref[...])
        o_ref[...] = x_ref[...] + _rms(m, gpost_ref[...])

    pl.when(pl.program_id(0) < n1)(lambda: emit(a1_ref, x1_ref))
    pl.when(pl.program_id(0) >= n1)(lambda: emit(a2_ref, x2_ref))


def _matmul_post_merged(a1, x1, a2, x2, w, g_post, tm):
    (t1, d), t2 = x1.shape, x2.shape[0]
    n1, n2 = t1 // tm, t2 // tm
    first = pl.BlockSpec((tm, d), lambda i: (jnp.minimum(i, n1 - 1), 0))
    second = pl.BlockSpec((tm, d), lambda i: (jnp.maximum(i - n1, 0), 0))
    return pl.pallas_call(
        functools.partial(_matmul_post_kernel, n1=n1),
        grid=(n1 + n2,),
        in_specs=[first, first, second, second,
                  pl.BlockSpec((d, d), lambda i: (0, 0)), pl.BlockSpec((1, d), lambda i: (0, 0))],
        out_specs=pl.BlockSpec((tm, d), lambda i: (i, 0)),
        out_shape=jax.ShapeDtypeStruct((t1 + t2, d), F32),
        compiler_params=_cparams("parallel"),
        name="matmul_post",
    )(a1, x1, a2, x2, w, g_post)


def _ffn_kernel(x_ref, gpre_ref, wg_ref, wu_ref, wd_ref, gpost_ref, o_ref, xn_scr, acc_scr):
    f, nf = pl.program_id(1), pl.num_programs(1)

    @pl.when(f == 0)
    def _():
        xn_scr[...] = _rms(x_ref[...], gpre_ref[...]).astype(BF16)
        acc_scr[...] = jnp.zeros_like(acc_scr)

    xn = xn_scr[...]
    h = (jax.nn.silu(_dot(xn, wg_ref[...])) * _dot(xn, wu_ref[...])).astype(BF16)
    acc_scr[...] += _dot(h, wd_ref[...])

    @pl.when(f == nf - 1)
    def _():
        o_ref[...] = x_ref[...] + _rms(acc_scr[...], gpost_ref[...])


def _ffn(x, g_pre, w_gate, w_up, w_down, g_post, tm, tf):
    t, d = x.shape
    ff = w_gate.shape[1]
    row = pl.BlockSpec((tm, d), lambda i, f: (i, 0))
    vec = pl.BlockSpec((1, d), lambda i, f: (0, 0))
    return pl.pallas_call(
        _ffn_kernel,
        grid=(t // tm, ff // tf),
        in_specs=[row, vec,
                  pl.BlockSpec((d, tf), lambda i, f: (0, f)),
                  pl.BlockSpec((d, tf), lambda i, f: (0, f)),
                  pl.BlockSpec((tf, d), lambda i, f: (f, 0)),
                  vec],
        out_specs=row,
        out_shape=jax.ShapeDtypeStruct((t, d), F32),
        scratch_shapes=[pltpu.VMEM((tm, d), BF16), pltpu.VMEM((tm, d), F32)],
        compiler_params=_cparams("parallel", "arbitrary"),
        name="ffn_dense",
    )(x, g_pre, w_gate, w_up, w_down, g_post)


META_G1, META_G2, META_E1, META_E2, META_R1, META_R2 = range(6)


def _router_kernel(x_ref, gpre_ref, wr_ref, br_ref, meta_ref, cnt_ref, cnt_scr):
    i = pl.program_id(0)

    @pl.when(i == 0)
    def _():
        cnt_scr[...] = jnp.zeros_like(cnt_scr)

    xn = _rms(x_ref[...], gpre_ref[...]).astype(BF16)
    logits = _dot(xn, wr_ref[...]) + br_ref[...]
    ex = jnp.exp(logits - jnp.max(logits, axis=-1, keepdims=True))
    probs = ex / jnp.sum(ex, axis=-1, keepdims=True)
    lane = lax.broadcasted_iota(jnp.int32, probs.shape, 1)
    p1 = jnp.max(probs, axis=-1, keepdims=True)
    i1 = jnp.min(jnp.where(probs == p1, lane, LANES), axis=-1, keepdims=True)
    rest = jnp.where(lane == i1, -1.0, probs)
    p2 = jnp.max(rest, axis=-1, keepdims=True)
    i2 = jnp.min(jnp.where(rest == p2, lane, LANES), axis=-1, keepdims=True)
    den = p1 + p2
    tm = probs.shape[0]
    chosen = (lane == i1) | (lane == i2)
    before = _tri01(tm, lambda r, c: c < r)
    seen = _dot(before, chosen.astype(BF16)) + cnt_scr[...]
    rank1 = jnp.sum(jnp.where(lane == i1, seen, 0.0), axis=-1, keepdims=True)
    rank2 = jnp.sum(jnp.where(lane == i2, seen, 0.0), axis=-1, keepdims=True)
    cnt_scr[...] += jnp.sum(chosen.astype(F32), axis=0, keepdims=True)
    cnt_ref[...] = cnt_scr[...]
    fields = {META_G1: p1 / den, META_G2: p2 / den, META_E1: i1.astype(F32), META_E2: i2.astype(F32),
              META_R1: rank1, META_R2: rank2}
    meta = jnp.zeros(probs.shape, F32)
    for k, v in fields.items():
        meta = jnp.where(lane == k, v, meta)
    meta_ref[...] = meta


def _router(x, g_pre, w_router, b_router, tm):
    t, d = x.shape
    ne = w_router.shape[1]
    wr = jnp.zeros((d, LANES), BF16).at[:, :ne].set(w_router.astype(BF16))
    br = jnp.full((1, LANES), -1e30, F32).at[0, :ne].set(b_router)
    return pl.pallas_call(
        _router_kernel,
        grid=(t // tm,),
        in_specs=[pl.BlockSpec((tm, d), lambda i: (i, 0)),
                  pl.BlockSpec((1, d), lambda i: (0, 0)),
                  pl.BlockSpec((d, LANES), lambda i: (0, 0)),
                  pl.BlockSpec((1, LANES), lambda i: (0, 0))],
        out_specs=[pl.BlockSpec((tm, LANES), lambda i: (i, 0)),
                   pl.BlockSpec((1, LANES), lambda i: (0, 0))],
        out_shape=[jax.ShapeDtypeStruct((t, LANES), F32), jax.ShapeDtypeStruct((1, LANES), F32)],
        scratch_shapes=[pltpu.VMEM((1, LANES), F32)],
        compiler_params=_cparams("arbitrary"),
        name="moe_router",
    )(x, g_pre, wr, br)


def _row_copy(src, src_row, dst, dst_row, sem):
    return pltpu.make_async_copy(src.at[pl.ds(src_row, 1)], dst.at[pl.ds(dst_row, 1)], sem)


def _dispatch_kernel(dst1_ref, dst2_ref, x_ref, zeros_hbm, xs_hbm, sem, *, tr):
    del zeros_hbm
    base = pl.program_id(0) * tr

    def each_row(fn):
        def body(r, _):
            t = base + r
            fn(_row_copy(x_ref, r, xs_hbm, dst1_ref[t], sem))
            fn(_row_copy(x_ref, r, xs_hbm, dst2_ref[t], sem))
            return 0

        lax.fori_loop(0, tr, body, 0, unroll=DMA_LOOP_UNROLL)

    each_row(lambda c: c.start())
    each_row(lambda c: c.wait())


def _dispatch(x, dst1, dst2, n_slots, tr):
    t, d = x.shape
    grid_spec = pltpu.PrefetchScalarGridSpec(
        num_scalar_prefetch=2,
        grid=(t // tr,),
        in_specs=[pl.BlockSpec((tr, d), lambda i, d1, d2: (i, 0)), pl.BlockSpec(memory_space=pl.ANY)],
        out_specs=pl.BlockSpec(memory_space=pl.ANY),
        scratch_shapes=[pltpu.SemaphoreType.DMA(())],
    )
    return pl.pallas_call(
        functools.partial(_dispatch_kernel, tr=tr),
        grid_spec=grid_spec,
        out_shape=jax.ShapeDtypeStruct((n_slots, d), F32),
        input_output_aliases={3: 0},
        compiler_params=_cparams("arbitrary"),
        name="moe_dispatch",
    )(dst1, dst2, x, jnp.zeros((n_slots, d), F32))


def _ffn_grouped_kernel(te_ref, tv_ref, x_ref, gpre_ref, wg_ref, wu_ref, wd_ref, y_ref, xn_scr):
    del te_ref
    i, f = pl.program_id(0), pl.program_id(1)
    valid = tv_ref[i] != 0

    @pl.when(f == 0)
    def _():
        xn_scr[...] = _rms(x_ref[...], gpre_ref[...]).astype(BF16)
        y_ref[...] = jnp.zeros_like(y_ref)

    @pl.when(valid)
    def _():
        xn = xn_scr[...]
        h = (jax.nn.silu(_dot(xn, wg_ref[...])) * _dot(xn, wu_ref[...])).astype(BF16)
        y_ref[...] += _dot(h, wd_ref[...])


def _ffn_grouped(xs, tile_expert, tile_valid, g_pre, w_gate, w_up, w_down, tm, tf):
    n_slots, d = xs.shape
    ff = w_gate.shape[2]
    row = pl.BlockSpec((tm, d), lambda i, f, te, tv: (i, 0))
    grid_spec = pltpu.PrefetchScalarGridSpec(
        num_scalar_prefetch=2,
        grid=(n_slots // tm, ff // tf),
        in_specs=[row,
                  pl.BlockSpec((1, d), lambda i, f, te, tv: (0, 0)),
                  pl.BlockSpec((None, d, tf), lambda i, f, te, tv: (te[i], 0, f)),
                  pl.BlockSpec((None, d, tf), lambda i, f, te, tv: (te[i], 0, f)),
                  pl.BlockSpec((None, tf, d), lambda i, f, te, tv: (te[i], f, 0))],
        out_specs=row,
        scratch_shapes=[pltpu.VMEM((tm, d), BF16)],
    )
    return pl.pallas_call(
        _ffn_grouped_kernel,
        grid_spec=grid_spec,
        out_shape=jax.ShapeDtypeStruct((n_slots, d), F32),
        compiler_params=_cparams("parallel", "arbitrary"),
        name="moe_ffn_grouped",
    )(tile_expert, tile_valid, xs, g_pre, w_gate, w_up, w_down)


def _combine_kernel(dst1_ref, dst2_ref, y_hbm, x_ref, meta_ref, gpost_ref, o_ref, ybuf, sems, *, tm, blk0):
    i, n = pl.program_id(0), pl.num_programs(0)

    def rows(step, slot, fn):
        base = (blk0 + step) * tm

        def body(r, _):
            t = base + r
            fn(_row_copy(y_hbm, dst1_ref[t], ybuf.at[slot, 0], r, sems.at[slot]))
            fn(_row_copy(y_hbm, dst2_ref[t], ybuf.at[slot, 1], r, sems.at[slot]))
            return 0

        lax.fori_loop(0, tm, body, 0, unroll=DMA_LOOP_UNROLL)

    @pl.when(i == 0)
    def _():
        rows(0, 0, lambda c: c.start())

    for slot in range(2):
        @pl.when((i + 1 < n) & ((i + 1) % 2 == slot))
        def _(slot=slot):
            rows(i + 1, slot, lambda c: c.start())

    for slot in range(2):
        @pl.when(i % 2 == slot)
        def _(slot=slot):
            rows(i, slot, lambda c: c.wait())
            meta = meta_ref[...]
            g1 = meta[:, META_G1:META_G1 + 1]
            g2 = meta[:, META_G2:META_G2 + 1]
            mix = g1 * ybuf[slot, 0] + g2 * ybuf[slot, 1]
            o_ref[...] = x_ref[...] + _rms(mix, gpost_ref[...])


def _combine(y, x, meta, dst1, dst2, g_post, tm, row0, n_rows):
    d = x.shape[1]
    blk0 = row0 // tm
    grid_spec = pltpu.PrefetchScalarGridSpec(
        num_scalar_prefetch=2,
        grid=(n_rows // tm,),
        in_specs=[pl.BlockSpec(memory_space=pl.ANY),
                  pl.BlockSpec((tm, d), lambda i, d1, d2: (blk0 + i, 0)),
                  pl.BlockSpec((tm, LANES), lambda i, d1, d2: (blk0 + i, 0)),
                  pl.BlockSpec((1, d), lambda i, d1, d2: (0, 0))],
        out_specs=pl.BlockSpec((tm, d), lambda i, d1, d2: (i, 0)),
        scratch_shapes=[pltpu.VMEM((2, 2, tm, d), F32), pltpu.SemaphoreType.DMA((2,))],
    )
    return pl.pallas_call(
        functools.partial(_combine_kernel, tm=tm, blk0=blk0),
        grid_spec=grid_spec,
        out_shape=jax.ShapeDtypeStruct((n_rows, d), F32),
        compiler_params=_cparams("arbitrary"),
        name="moe_combine",
    )(dst1, dst2, y, x, meta, g_post)


def _moe_plan(meta, counts, n_experts, tm):
    t = meta.shape[0]
    cnt = counts[0, :n_experts].astype(jnp.int32)
    padded = ((cnt + tm - 1) // tm) * tm
    ends = jnp.cumsum(padded)
    offs = ends - padded
    e1, e2 = meta[:, META_E1].astype(jnp.int32), meta[:, META_E2].astype(jnp.int32)
    dst1 = offs[e1] + meta[:, META_R1].astype(jnp.int32)
    dst2 = offs[e2] + meta[:, META_R2].astype(jnp.int32)
    n_tiles = (2 * t + n_experts * (tm - 1)) // tm
    starts = jnp.arange(n_tiles, dtype=jnp.int32) * tm
    tile_expert = jnp.minimum(jnp.sum(starts[:, None] >= ends[None, :], axis=1), n_experts - 1).astype(jnp.int32)
    tile_valid = (starts < ends[-1]).astype(jnp.int32)
    return dst1, dst2, tile_expert, tile_valid, n_tiles * tm


def _log_sigmoid(z):
    return jnp.minimum(z, 0.0) - jnp.log(1.0 + jnp.exp(-jnp.abs(z)))


def _fox_proj_prompt_kernel(x_ref, gpre_ref, wq_ref, wkvt_ref, wft_ref, bf_ref,
                            q_ref, kt_ref, vt_ref, lft_ref, *, scale):
    xn = _rms(x_ref[...], gpre_ref[...]).astype(BF16)
    q_ref[...] = (_dot(xn, wq_ref[...]) * scale).astype(BF16)
    d = kt_ref.shape[0]
    kvt = _dot_nt(wkvt_ref[...], xn)
    kt_ref[...] = kvt[:d]
    vt_ref[...] = kvt[d:]
    lft_ref[...] = _log_sigmoid(_dot_nt(wft_ref[...], xn) + bf_ref[...])


def _fox_proj_prompt(x, g_pre, wq, wkvt, wft, b_f, scale, tm):
    nb, nl, d = x.shape
    nh = wft.shape[0]
    const = lambda b, i: (0, 0)
    return pl.pallas_call(
        functools.partial(_fox_proj_prompt_kernel, scale=scale),
        grid=(nb, nl // tm),
        in_specs=[pl.BlockSpec((None, tm, d), lambda b, i: (b, i, 0)),
                  pl.BlockSpec((1, d), const),
                  pl.BlockSpec(wq.shape, const),
                  pl.BlockSpec(wkvt.shape, const),
                  pl.BlockSpec(wft.shape, const),
                  pl.BlockSpec((nh, 1), const)],
        out_specs=[pl.BlockSpec((None, tm, d), lambda b, i: (b, i, 0)),
                   pl.BlockSpec((None, d, tm), lambda b, i: (b, 0, i)),
                   pl.BlockSpec((None, d, tm), lambda b, i: (b, 0, i)),
                   pl.BlockSpec((None, nh, tm), lambda b, i: (b, 0, i))],
        out_shape=[jax.ShapeDtypeStruct((nb, nl, d), BF16),
                   jax.ShapeDtypeStruct((nb, d, nl), F32),
                   jax.ShapeDtypeStruct((nb, d, nl), F32),
                   jax.ShapeDtypeStruct((nb, nh, nl), F32)],
        compiler_params=_cparams("parallel", "parallel"),
        name="fox_proj_prompt",
    )(x, g_pre, wq, wkvt, wft, b_f)


def _tri01(n, rel):
    r = lax.broadcasted_iota(jnp.int32, (n, n), 0)
    c = lax.broadcasted_iota(jnp.int32, (n, n), 1)
    return rel(r, c).astype(BF16)


def _cumsum_lanes_kernel(x_ref, o_ref):
    upper = _tri01(LANES, lambda r, c: r <= c)
    carry = jnp.zeros((x_ref.shape[0], 1), F32)
    for blk in range(x_ref.shape[1] // LANES):
        sl = slice(blk * LANES, (blk + 1) * LANES)
        cum = _dot_exact_rhs01(x_ref[:, sl], upper) + carry
        o_ref[:, sl] = cum
        carry = cum[:, LANES - 1:LANES]


def _cumsum_lanes(x):
    nb, r, nl = x.shape
    spec = pl.BlockSpec((None, r, nl), lambda b: (b, 0, 0))
    return pl.pallas_call(
        _cumsum_lanes_kernel, grid=(nb,), in_specs=[spec], out_specs=spec,
        out_shape=jax.ShapeDtypeStruct(x.shape, F32),
        compiler_params=_cparams("parallel"), name="logf_cumsum",
    )(x)


def _prompt_attn_part(q_ref, kt_ref, vt_ref, c_ref, o_ref, m_scr, acc_scr, hpair, qi, phase, *, tq, tk, hd):
    heads = LANES // hd
    ones_rows = jnp.ones((acc_scr.shape[2] - hd, tk), BF16)
    q0 = pl.multiple_of(qi * tq, tq)
    n_full = q0 // tk
    row = lax.broadcasted_iota(jnp.int32, (tq, tk), 0)
    col = lax.broadcasted_iota(jnp.int32, (tq, tk), 1)
    hslice = [slice(hh * hd, (hh + 1) * hd) for hh in range(heads)]
    qs = [q_ref[pl.ds(q0, tq), hs] for hs in hslice]

    def tile(k0, carry, masked):
        out = []
        for hh, hs in enumerate(hslice):
            m, acc = carry[hh]
            kt = kt_ref[hs, pl.ds(k0, tk)].astype(BF16)
            vt = jnp.concatenate([vt_ref[hs, pl.ds(k0, tk)].astype(BF16), ones_rows], axis=0)
            cc = c_ref[pl.ds(hpair * heads + hh, 1), pl.ds(k0, tk)]
            s = _dot(qs[hh], kt) - cc
            if masked:
                s = jnp.where(q0 + row >= k0 + col, s, NEG_INF)
            blocks = [s[:, j * LANES:(j + 1) * LANES] for j in range(tk // LANES)]
            m_new = jnp.maximum(m, jnp.max(functools.reduce(jnp.maximum, blocks), axis=-1, keepdims=True))
            alpha = jnp.exp(m - m_new)
            p = jnp.concatenate([jnp.exp(blk - m_new).astype(BF16) for blk in blocks], axis=1)
            acc = alpha[:, :acc.shape[1]] * acc + _dot_nt(p, vt)
            out.append((m_new, acc))
        return tuple(out)

    n_first = (n_full + 1) // 2
    first = phase == 0
    init = tuple((jnp.where(first, NEG_INF, m_scr[hh]), jnp.where(first, 0.0, acc_scr[hh]))
                 for hh in range(heads))
    lo = jnp.where(first, 0, n_first)
    carry = lax.fori_loop(0, jnp.where(first, n_first, n_full - n_first),
                          lambda j, cr: tile(pl.multiple_of((lo + j) * tk, tk), cr, False), init)

    @pl.when(first)
    def _():
        for hh in range(heads):
            m_scr[hh], acc_scr[hh] = carry[hh]

    @pl.when(phase == 1)
    def _():
        done = tile(pl.multiple_of(n_full * tk, tk), carry, True)
        for hh, hs in enumerate(hslice):
            _, acc = done[hh]
            o_ref[:, hs] = (acc[:, :hd] / acc[:, hd:hd + 1]).astype(o_ref.dtype)


def _fox_proj_sample_kernel(x_ref, gpre_ref, w_ref, wf_ref, bf_ref, q_ref, k_ref, v_ref, lf_ref, *, scale):
    xn = _rms(x_ref[...], gpre_ref[...]).astype(BF16)
    d = q_ref.shape[1]
    proj = _dot(xn, w_ref[...])
    q_ref[...] = proj[:, :d] * scale
    k_ref[...] = proj[:, d:2 * d]
    v_ref[...] = proj[:, 2 * d:]
    lf_ref[...] = _log_sigmoid(_dot(xn, wf_ref[...]) + bf_ref[...])


def _fox_proj_sample(x, g_pre, w_qkv, w_f, b_f, scale):
    t, d = x.shape
    nh = w_f.shape[1]
    wf = jnp.zeros((d, LANES), BF16).at[:, :nh].set(w_f)
    bf = jnp.zeros((1, LANES), F32).at[0, :nh].set(b_f)
    full = lambda a: pl.BlockSpec(a.shape, lambda i: (0,) * a.ndim)
    row = pl.BlockSpec((t, d), lambda i: (0, 0))
    return pl.pallas_call(
        functools.partial(_fox_proj_sample_kernel, scale=scale),
        grid=(1,),
        in_specs=[row, full(g_pre), full(w_qkv), full(wf), full(bf)],
        out_specs=[row, row, row, pl.BlockSpec((t, LANES), lambda i: (0, 0))],
        out_shape=[jax.ShapeDtypeStruct((t, d), F32)] * 3 + [jax.ShapeDtypeStruct((t, LANES), F32)],
        compiler_params=_cparams("arbitrary"),
        name="fox_proj_sample",
    )(x, g_pre, w_qkv, wf, bf)


def _paged_attn_part(q_ref, kc_refs, vc_refs, lfc_refs, kn_ref, vn_ref, lfn_ref, o_ref,
                     qbd, m_scr, l_scr, acc_scr, carry, bias_scr, p, n_chunks, *, ns, hd):
    gp = len(kc_refs)
    n_groups, g_rows, g_width = qbd.shape
    rows, d = n_groups * g_rows, n_groups * g_width
    nh = d // hd
    gmask = (lax.broadcasted_iota(jnp.int32, (g_rows, g_width), 0) // ns
             == lax.broadcasted_iota(jnp.int32, (g_rows, g_width), 1) // hd)
    expand = (lax.broadcasted_iota(jnp.int32, (rows, nh), 0) // ns
              == lax.broadcasted_iota(jnp.int32, (rows, nh), 1)).astype(BF16)
    g_cols = lambda g: slice(g * g_width, (g + 1) * g_width)
    g_rws = lambda g: slice(g * g_rows, (g + 1) * g_rows)

    @pl.when(p == 0)
    def _():
        for g in range(n_groups):
            qt = jnp.concatenate([q_ref[:, g_cols(g)]] * (g_rows // ns), axis=0)
            qbd[g] = jnp.where(gmask, qt, 0.0).astype(BF16)
        m_scr[...] = jnp.full_like(m_scr, NEG_INF)
        l_scr[...] = jnp.zeros_like(l_scr)
        acc_scr[...] = jnp.zeros_like(acc_scr)
        carry[...] = jnp.zeros_like(carry)

    def per_group(fn):
        return jnp.concatenate([fn(g) for g in range(n_groups)], axis=0)

    def update(s, pv_of_group):
        m_new = jnp.maximum(m_scr[...], jnp.max(s, axis=-1, keepdims=True))
        alpha = jnp.exp(m_scr[...] - m_new)
        pr = jnp.exp(s - m_new)
        l_scr[...] = alpha * l_scr[...] + jnp.sum(pr, axis=-1, keepdims=True)
        pr = pr.astype(BF16)
        acc_scr[...] = alpha * acc_scr[...] + per_group(lambda g: pv_of_group(g, pr[g_rws(g)]))
        m_scr[...] = m_new

    lfs = [r[...] for r in lfc_refs]
    after = _tri01(LANES, lambda r, c: r > c)
    within = _dot_exact_rhs01(jnp.concatenate(lfs, axis=0), after)
    run = carry[...]
    for k in reversed(range(gp)):
        w_k = within[k * nh:(k + 1) * nh]
        bias_scr[:, k * LANES:(k + 1) * LANES] = run + w_k
        run = run + (w_k[:, 0:1] + lfs[k][:, 0:1])
    carry[...] = run
    def pages(refs, g):
        return jnp.concatenate([r[g_cols(g), :].astype(BF16) for r in refs], axis=1)

    s = per_group(lambda g: _dot(qbd[g], pages(kc_refs, g)))
    s = jnp.concatenate([s[h * ns:(h + 1) * ns] + bias_scr[h:h + 1, :] for h in range(nh)], axis=0)
    update(s, lambda g, pr: _dot_nt(pr, pages(vc_refs, g)))

    @pl.when(p == n_chunks - 1)
    def _():
        upto = _tri01(LANES, lambda r, c: r <= c)
        cn = _dot_exact_rhs01(lfn_ref[...], upto)
        pad = jnp.zeros((LANES - ns, d), F32)
        kn = jnp.concatenate([kn_ref[...], pad], axis=0).astype(BF16)
        vn = jnp.concatenate([vn_ref[...], pad], axis=0).astype(BF16)
        sn = per_group(lambda g: _dot_nt(qbd[g], kn[:, g_cols(g)])) - _dot_exact_lhs01(expand, cn)
        qpos = lax.broadcasted_iota(jnp.int32, (rows, LANES), 0) % ns
        kpos = lax.broadcasted_iota(jnp.int32, (rows, LANES), 1)
        update(jnp.where(kpos <= qpos, sn, NEG_INF), lambda g, pr: _dot(pr, vn[:, g_cols(g)]))
        o = acc_scr[...] / l_scr[...]
        for g in range(n_groups):
            og = jnp.where(gmask, o[g_rws(g)], 0.0)
            o_ref[:, g_cols(g)] = functools.reduce(
                jnp.add, [og[h * ns:(h + 1) * ns] for h in range(g_rows // ns)])


def _fox_attn_kernel(tbl_ref, *refs, gp, n_chunks, n_sample_steps, n_hpairs, n_qtiles, n_prompt_steps,
                     ns, hd, tq, tk):
    del tbl_ref
    sq_ref, refs = refs[0], refs[1:]
    kc_refs, vc_refs, lfc_refs = refs[:gp], refs[gp:2 * gp], refs[2 * gp:3 * gp]
    kn_ref, vn_ref, lfn_ref, q_ref, kt_ref, vt_ref, c_ref, os_ref, op_ref = refs[3 * gp:3 * gp + 9]
    decode_scratch, prompt_scratch = refs[3 * gp + 9:-2], refs[-2:]
    s = pl.program_id(0)

    @pl.when(s == 0)
    def _():
        for ref in prompt_scratch:
            ref[...] = jnp.zeros_like(ref)

    @pl.when(s < n_sample_steps)
    def _():
        _paged_attn_part(sq_ref, kc_refs, vc_refs, lfc_refs, kn_ref, vn_ref, lfn_ref, os_ref, *decode_scratch,
                         s % n_chunks, n_chunks, ns=ns, hd=hd)

    @pl.when(s < n_prompt_steps)
    def _():
        unit = s // 2
        _prompt_attn_part(q_ref, kt_ref, vt_ref, c_ref, op_ref, *prompt_scratch,
                          (unit // n_qtiles) % n_hpairs, unit % n_qtiles, s % 2, tq=tq, tk=tk, hd=hd)


def _fox_attn(page_table, sq, kc, vc, lfc, kn, vn, lfn, q, kt, vt, c, hd):
    db, ns, d = sq.shape
    n_pages = page_table.shape[1]
    nh = d // hd
    rows = nh * ns
    page = kc.shape[2]
    assert rows == LANES and page == LANES
    gp = next(g for g in (8, 4, 2, 1) if n_pages % g == 0)
    n_chunks = n_pages // gp
    n_sample_steps = db * n_chunks

    nb, nl, _ = q.shape
    tq = tk = min(nl, 512)
    n_hpairs, n_qtiles = d // LANES, nl // tq
    n_prompt_steps = nb * n_hpairs * n_qtiles * 2

    def sample_pos(s):
        s = jnp.minimum(s, n_sample_steps - 1)
        return s // n_chunks, s % n_chunks

    def prompt_pos(s):
        s = jnp.minimum(s, n_prompt_steps - 1)
        unit = s // 2
        return unit // (n_hpairs * n_qtiles), (unit // n_qtiles) % n_hpairs, unit % n_qtiles

    seq = pl.BlockSpec((None, ns, d), lambda s, tbl: (sample_pos(s)[0], 0, 0))

    def pg(k):
        def index(s, tbl):
            b, p = sample_pos(s)
            return tbl[b, (n_chunks - 1 - p) * gp + k], 0, 0
        return index

    def pidx(fn):
        return lambda s, tbl: fn(*prompt_pos(s))

    grid_spec = pltpu.PrefetchScalarGridSpec(
        num_scalar_prefetch=1,
        grid=(max(n_sample_steps, n_prompt_steps),),
        in_specs=([seq]
                  + [pl.BlockSpec((None, d, page), pg(k)) for k in range(gp)]
                  + [pl.BlockSpec((None, d, page), pg(k)) for k in range(gp)]
                  + [pl.BlockSpec((None, nh, page), pg(k)) for k in range(gp)]
                  + [seq, seq, pl.BlockSpec((None, nh, LANES), lambda s, tbl: (sample_pos(s)[0], 0, 0))]
                  + [pl.BlockSpec((None, nl, LANES), pidx(lambda b, h, i: (b, 0, h))),
                     pl.BlockSpec((None, LANES, nl), pidx(lambda b, h, i: (b, h, 0))),
                     pl.BlockSpec((None, LANES, nl), pidx(lambda b, h, i: (b, h, 0))),
                     pl.BlockSpec((None, nh, nl), pidx(lambda b, h, i: (b, 0, 0)))]),
        out_specs=[seq, pl.BlockSpec((None, tq, LANES), pidx(lambda b, h, i: (b, i, h)))],
        scratch_shapes=[pltpu.VMEM((d // V7X_MXU_DIM, rows * V7X_MXU_DIM // d, V7X_MXU_DIM), BF16),
                        pltpu.VMEM((rows, 1), F32),
                        pltpu.VMEM((rows, 1), F32),
                        pltpu.VMEM((rows, V7X_MXU_DIM), F32),
                        pltpu.VMEM((nh, 1), F32),
                        pltpu.VMEM((nh, gp * page), F32),
                        pltpu.VMEM((LANES // hd, tq, LANES), F32),
                        pltpu.VMEM((LANES // hd, tq, hd + 2 * SUBLANES), F32)],
    )
    return pl.pallas_call(
        functools.partial(_fox_attn_kernel, gp=gp, n_chunks=n_chunks, n_sample_steps=n_sample_steps,
                          n_hpairs=n_hpairs, n_qtiles=n_qtiles, n_prompt_steps=n_prompt_steps,
                          ns=ns, hd=hd, tq=tq, tk=tk),
        grid_spec=grid_spec,
        out_shape=[jax.ShapeDtypeStruct((db, ns, d), F32), jax.ShapeDtypeStruct((nb, nl, d), BF16)],
        compiler_params=_cparams("arbitrary"),
        name="fox_attn",
    )(page_table, sq, *([kc] * gp), *([vc] * gp), *([lfc] * gp), kn, vn, lfn, q, kt, vt, c)


def _div_tile(t, candidates):
    return next((tm for tm in candidates if t % tm == 0), t)


def _row_tile(t):
    return _div_tile(t, (1024, 512, 256, 128))


def _ff_tile(ff):
    return _div_tile(ff, (7 * V7X_MXU_DIM, 4 * V7X_MXU_DIM, 2 * V7X_MXU_DIM, V7X_MXU_DIM, LANES))


def kernel(x_prompt, x_sample, state_ssm_re, state_ssm_im, cache_k, cache_v, cache_logf, page_table, norm_mix_pre, norm_mix_post, norm_ffn_pre, norm_ffn_post, ssm_w_in, ssm_lambda_re, ssm_lambda_im, ssm_log_dt, ssm_b_re, ssm_b_im, ssm_c_re, ssm_c_im, ssm_d, ssm_w_glu, ssm_b_glu, ssm_w_out, fox_w_in, fox_b_f, fox_w_out, ffn_w_gate, ffn_w_up, ffn_w_down, moe_w_router, moe_b_router, moe_w_gate, moe_w_up, moe_w_down):
    nb, nl, d = x_prompt.shape
    db, ns, _ = x_sample.shape
    n_groups, n_state = ssm_lambda_re.shape[1:]
    nh = fox_b_f.shape[1]
    hd = d // nh
    scale = hd ** -0.5
    ng = d // LANES
    streams = [x_prompt.reshape(nb * nl, d), x_sample.reshape(db * ns, d)]
    dims = [(nb, nl), (db, ns)]
    tms = [_row_tile(x.shape[0]) for x in streams]
    vec = lambda a: a.reshape(1, -1)

    li = 0
    w_in = ssm_w_in[li].astype(BF16)
    w_glu = ssm_w_glu[li].astype(BF16)
    w_out = ssm_w_out[li].astype(BF16)
    s5w = _s5_prepare(ssm_lambda_re[li], ssm_lambda_im[li], ssm_log_dt[li], ssm_b_re[li], ssm_b_im[li],
                      ssm_c_re[li], ssm_c_im[li])
    wg, wu, wd = (w[li].astype(BF16) for w in (ffn_w_gate, ffn_w_up, ffn_w_down))
    h0s = [jnp.zeros((ng, nb, 2 * (n_groups // ng) * n_state), F32),
           _s5_state_in(state_ssm_re[li], state_ssm_im[li], ng)]
    ssm_states = []
    for si in range(2):
        x, (b_, l_), tm = streams[si], dims[si], tms[si]
        u = _norm_matmul(x, vec(norm_mix_pre[0]), w_in, tm)
        g_act, h_fin = _s5_core(u.reshape(b_, l_, d), h0s[si], *s5w, vec(ssm_d[li]))
        ssm_states.append(_s5_state_out(h_fin, n_groups, n_state))
        x = _glu_out(g_act.reshape(b_ * l_, d), x, w_glu, vec(ssm_b_glu[li]), w_out, vec(norm_mix_post[0]), tm)
        streams[si] = _ffn(x, vec(norm_ffn_pre[0]), wg, wu, wd, vec(norm_ffn_post[0]), min(tm, FFN_ROW_TILE),
                           _ff_tile(wg.shape[1]))

    w_fox = fox_w_in[li]
    w_fox_t = w_fox.T
    wq = w_fox[:, :d].astype(BF16)
    wkvt = w_fox_t[d:3 * d].astype(BF16)
    wft = w_fox_t[3 * d:].astype(BF16)
    w_o = fox_w_out[li].astype(BF16)

    xp = streams[0]
    q, kt, vt, lft = _fox_proj_prompt(xp.reshape(nb, nl, d), vec(norm_mix_pre[1]), wq, wkvt, wft,
                                      fox_b_f[li].reshape(nh, 1), scale, min(nl, 1024))
    c = _cumsum_lanes(lft)
    tp, tsm = nb * nl, db * ns
    k_prompt = kt.reshape(nb, nh, hd, nl).transpose(0, 3, 1, 2)[None]
    v_prompt = vt.reshape(nb, nh, hd, nl).transpose(0, 3, 1, 2)[None]
    logf_prompt = lft.transpose(0, 2, 1)[None]

    xs = streams[1]
    qs, ks, vs, lfs = _fox_proj_sample(xs, vec(norm_mix_pre[1]), w_fox[:, :3 * d].astype(BF16),
                                       w_fox[:, 3 * d:].astype(BF16), fox_b_f[li], scale)
    n_pool, page = cache_k.shape[1], cache_k.shape[2]
    kc = cache_k[li].transpose(0, 2, 3, 1).reshape(n_pool, d, page)
    vc = cache_v[li].transpose(0, 2, 3, 1).reshape(n_pool, d, page)
    lfc = cache_logf[li].transpose(0, 2, 1)
    lfs = lfs[:, :nh]
    lfn = jnp.zeros((db, nh, LANES), F32).at[:, :, :ns].set(lfs.reshape(db, ns, nh).transpose(0, 2, 1))
    os_, o = _fox_attn(page_table, qs.reshape(db, ns, d), kc, vc, lfc,
                       ks.reshape(db, ns, d), vs.reshape(db, ns, d), lfn, q, kt, vt, c, hd)
    small = (256, 128, 64, 32, 16, 8)
    x_all = _matmul_post_merged(o.reshape(tp, d), xp, os_.reshape(tsm, d), xs, w_o, vec(norm_mix_post[1]),
                                _div_tile(tsm, small))
    k_sample = ks.reshape(1, db, ns, nh, hd)
    v_sample = vs.reshape(1, db, ns, nh, hd)
    logf_sample = lfs.reshape(1, db, ns, nh)

    mg, mu, md = (w[li].astype(BF16) for w in (moe_w_gate, moe_w_up, moe_w_down))
    n_experts = mg.shape[0]
    t_route = _div_tile(tp + tsm, small)
    meta, counts = _router(x_all, vec(norm_ffn_pre[1]), moe_w_router[li], moe_b_router[li], t_route)
    dst1, dst2, tile_expert, tile_valid, n_slots = _moe_plan(meta, counts, n_experts, FFN_ROW_TILE)
    x_sorted = _dispatch(x_all, dst1, dst2, n_slots, t_route)
    y_sorted = _ffn_grouped(x_sorted, tile_expert, tile_valid, vec(norm_ffn_pre[1]), mg, mu, md, FFN_ROW_TILE,
                            _ff_tile(mg.shape[2]))
    outs = [_combine(y_sorted, x_all, meta, dst1, dst2, vec(norm_ffn_post[1]), _div_tile(n, small), r0, n)
            for r0, n in ((0, tp), (tp, tsm))]

    (re_p, im_p), (re_s, im_s) = ssm_states
    return (outs[0].reshape(nb, nl, d), outs[1].reshape(db, ns, d),
            re_p[None], im_p[None], re_s[None], im_s[None],
            k_prompt, v_prompt, logf_prompt, k_sample, v_sample, logf_sample)
```

```python
import functools

import jax
import jax.numpy as jnp
from jax import lax
from jax.experimental import pallas as pl
from jax.experimental.pallas import tpu as pltpu

F32 = jnp.float32
BF16 = jnp.bfloat16
RMS_EPS = 1e-6
LANES = 128
SUBLANES = 8
V7X_MXU_DIM = 256
V7X_VMEM_LIMIT_BYTES = 56 << 20
S5_CHUNK = 8
S5_SCAN_UNROLL = 8
FFN_ROW_TILE = 512
DMA_LOOP_UNROLL = 8
NEG_INF = float("-inf")


def _cparams(*sem):
    return pltpu.CompilerParams(dimension_semantics=sem, vmem_limit_bytes=V7X_VMEM_LIMIT_BYTES)


def _rms(x, g):
    return x * lax.rsqrt(jnp.mean(x * x, axis=-1, keepdims=True) + RMS_EPS) * g


def _dot(a, b):
    return jnp.dot(a, b, preferred_element_type=F32)


def _dot_nt(a, b):
    return lax.dot_general(a, b, (((1,), (1,)), ((), ())), preferred_element_type=F32)


def _split3(x):
    hi = x.astype(BF16)
    r = x - hi.astype(F32)
    mid = r.astype(BF16)
    lo = (r - mid.astype(F32)).astype(BF16)
    return hi, mid, lo


def _dot_exact_rhs01(x, m01):
    hi, mid, lo = _split3(x)
    return _dot(hi, m01) + _dot(mid, m01) + _dot(lo, m01)


def _dot_exact_lhs01(m01, x):
    hi, mid, lo = _split3(x)
    return _dot(m01, hi) + _dot(m01, mid) + _dot(m01, lo)


def _norm_matmul_kernel(x_ref, g_ref, w_ref, o_ref):
    xn = _rms(x_ref[...], g_ref[...]).astype(BF16)
    o_ref[...] = _dot(xn, w_ref[...])


def _norm_matmul(x, g, w, tm):
    t, d = x.shape
    n = w.shape[1]
    return pl.pallas_call(
        _norm_matmul_kernel,
        grid=(t // tm,),
        in_specs=[pl.BlockSpec((tm, d), lambda i: (i, 0)),
                  pl.BlockSpec((1, d), lambda i: (0, 0)),
                  pl.BlockSpec((d, n), lambda i: (0, 0))],
        out_specs=pl.BlockSpec((tm, n), lambda i: (i, 0)),
        out_shape=jax.ShapeDtypeStruct((t, n), F32),
        compiler_params=_cparams("parallel"),
        name="norm_matmul",
    )(x, g, w)


def _s5_prepare(lam_re, lam_im, log_dt, b_re, b_im, c_re, c_im):
    tc = S5_CHUNK
    g, p = lam_re.shape
    c = b_re.shape[-1]
    ng = (g * c) // LANES
    gl = g // ng
    hp = lax.Precision.HIGHEST
    dt = jnp.exp(log_dt)[:, None]
    mag = jnp.exp(lam_re * dt)
    ar = mag * jnp.cos(lam_im * dt)
    ai = mag * jnp.sin(lam_im * dt)
    den = lam_re * lam_re + lam_im * lam_im
    nr = ar - 1.0
    coef_r = (nr * lam_re + ai * lam_im) / den
    coef_i = (ai * lam_re - nr * lam_im) / den
    bbr = coef_r[..., None] * b_re - coef_i[..., None] * b_im
    bbi = coef_r[..., None] * b_im + coef_i[..., None] * b_re
    prs, pis = [jnp.ones_like(ar)], [jnp.zeros_like(ar)]
    for _ in range(tc):
        nr_, ni_ = prs[-1] * ar - pis[-1] * ai, prs[-1] * ai + pis[-1] * ar
        prs.append(nr_)
        pis.append(ni_)
    pr = jnp.stack(prs)
    pi = jnp.stack(pis)
    prs_ = pr[tc - 1 - jnp.arange(tc)][..., None]
    pis_ = pi[tc - 1 - jnp.arange(tc)][..., None]
    ms = jnp.stack([prs_ * bbr - pis_ * bbi, prs_ * bbi + pis_ * bbr])
    ms = ms.transpose(2, 1, 4, 0, 3).reshape(ng, gl, tc, c, 2 * p)
    m_state = ms.transpose(0, 2, 1, 3, 4).reshape(ng, tc * gl * c, 2 * p)

    er = pr[:tc, :, :, None] * bbr - pi[:tc, :, :, None] * bbi
    ei = pr[:tc, :, :, None] * bbi + pi[:tc, :, :, None] * bbr
    kk = (jnp.einsum("gdp,kgpc->kgdc", c_re, er, precision=hp)
          - jnp.einsum("gdp,kgpc->kgdc", c_im, ei, precision=hp))
    lag = jnp.arange(tc)[None, :] - jnp.arange(tc)[:, None]
    kt = jnp.where((lag >= 0)[:, :, None, None, None], kk[jnp.clip(lag, 0)], 0.0)
    kt = kt.transpose(2, 0, 4, 1, 3).reshape(ng, gl, tc, c, tc * c)
    m_intra = kt.transpose(0, 2, 1, 3, 4).reshape(ng, tc * gl * c, tc * c)

    pr1 = pr[1:].transpose(1, 2, 0)[:, :, :, None]
    pi1 = pi[1:].transpose(1, 2, 0)[:, :, :, None]
    cre = c_re.transpose(0, 2, 1)[:, :, None, :]
    cim = c_im.transpose(0, 2, 1)[:, :, None, :]
    hh = jnp.stack([cre * pr1 - cim * pi1, -(cre * pi1 + cim * pr1)])
    m_carry = hh.reshape(2, ng, gl * p, tc * c).transpose(1, 0, 2, 3).reshape(ng, 2 * gl * p, tc * c)

    w_state = _s5_expand(m_state, gl, row_group=c, col_block=gl * p, col_sub=p)
    w_intra = _s5_expand(m_intra, gl, row_group=c, col_block=gl * c, col_sub=c)
    w_carry = _s5_expand(m_carry, gl, row_group=p, col_block=gl * c, col_sub=c)
    a_pow = jnp.concatenate([pr[tc].reshape(ng, 1, gl * p), pi[tc].reshape(ng, 1, gl * p)], axis=-1)
    return w_state, w_intra, w_carry, a_pow


def _s5_expand_kernel(m_ref, o_ref, *, gl, row_group, col_block, col_sub):
    rows, cols = o_ref.shape
    s_i = lax.broadcasted_iota(jnp.int32, (m_ref.shape[1], cols), 0)
    j_i = lax.broadcasted_iota(jnp.int32, (m_ref.shape[1], cols), 1)
    place = (s_i == (j_i // col_block) * col_sub + j_i % col_sub).astype(BF16)
    spread = _dot(m_ref[...].astype(BF16), place)
    r_g = (lax.broadcasted_iota(jnp.int32, (rows, cols), 0) // row_group) % gl
    c_g = (lax.broadcasted_iota(jnp.int32, (rows, cols), 1) % col_block) // col_sub
    o_ref[...] = jnp.where(r_g == c_g, spread, 0.0).astype(o_ref.dtype)


def _s5_expand(m, gl, row_group, col_block, col_sub):
    ng, rows, k = m.shape
    cols = k * gl
    return pl.pallas_call(
        functools.partial(_s5_expand_kernel, gl=gl, row_group=row_group, col_block=col_block, col_sub=col_sub),
        grid=(ng,),
        in_specs=[pl.BlockSpec((None, rows, k), lambda a: (a, 0, 0))],
        out_specs=pl.BlockSpec((None, rows, cols), lambda a: (a, 0, 0)),
        out_shape=jax.ShapeDtypeStruct((ng, rows, cols), BF16),
        compiler_params=_cparams("parallel"),
        name="s5_expand",
    )(m)


def _s5_core_kernel(u_ref, h0_ref, ws_ref, wi_ref, wh_ref, ap_ref, d_ref, g_ref, hout_ref,
                    s_scr, hs_scr, hcar, *, nb, nj):
    tc = S5_CHUNK
    rows = nb * nj
    nk = s_scr.shape[0]
    half = nk // 2
    lane_blk = lambda k: slice(k * LANES, (k + 1) * LANES)

    @pl.when(pl.program_id(1) == 0)
    def _():
        hcar[...] = h0_ref[...]

    def u_at(t):
        if nj == 1:
            return u_ref[:, t, :]
        return u_ref[:, pl.ds(t, nj, stride=tc), :].reshape(rows, LANES)

    us = [u_at(t) for t in range(tc)]
    x = jnp.concatenate([v.astype(BF16) for v in us], axis=1)
    s_all = _dot(x, ws_ref[...])
    for k in range(nk):
        s_scr[k] = s_all[:, lane_blk(k)]

    a_pow = [ap_ref[:, lane_blk(k)] for k in range(nk)]

    def rows_at(j):
        return slice(None) if nj == 1 else pl.ds(j, nb, stride=nj)

    def step(j, h):
        for k in range(nk):
            hs_scr[k, rows_at(j), :] = h[k]
        new_re, new_im = [], []
        for k in range(half):
            hr, hi, ar, ai = h[k], h[k + half], a_pow[k], a_pow[k + half]
            new_re.append(ar * hr - ai * hi + s_scr[k, rows_at(j), :])
            new_im.append(ar * hi + ai * hr + s_scr[k + half, rows_at(j), :])
        return tuple(new_re + new_im)

    h = tuple(hcar[:, lane_blk(k)] for k in range(nk))
    h = step(0, h) if nj == 1 else lax.fori_loop(0, nj, step, h, unroll=S5_SCAN_UNROLL)
    for k in range(nk):
        hcar[:, lane_blk(k)] = h[k]
        hout_ref[:, lane_blk(k)] = h[k]

    hs = jnp.concatenate([hs_scr[k].astype(BF16) for k in range(nk)], axis=1)
    y = _dot(x, wi_ref[...]) + _dot(hs, wh_ref[...])
    d = d_ref[...]
    for t in range(tc):
        gt = jax.nn.gelu(y[:, lane_blk(t)] + d * us[t])
        if nj == 1:
            g_ref[:, t, :] = gt
        else:
            g_ref[:, pl.ds(t, nj, stride=tc), :] = gt.reshape(nb, nj, LANES)


def _s5_core(u, h0, w_state, w_intra, w_carry, a_pow, d_skip):
    nb, nl, d = u.shape
    ng = d // LANES
    ts = min(nl, 512)
    nj = ts // S5_CHUNK
    sw = w_state.shape[2]
    kern = functools.partial(_s5_core_kernel, nb=nb, nj=nj)
    return pl.pallas_call(
        kern,
        grid=(ng, nl // ts),
        in_specs=[pl.BlockSpec((nb, ts, LANES), lambda g, s: (0, s, g)),
                  pl.BlockSpec((None, nb, sw), lambda g, s: (g, 0, 0)),
                  pl.BlockSpec((None,) + w_state.shape[1:], lambda g, s: (g, 0, 0)),
                  pl.BlockSpec((None,) + w_intra.shape[1:], lambda g, s: (g, 0, 0)),
                  pl.BlockSpec((None,) + w_carry.shape[1:], lambda g, s: (g, 0, 0)),
                  pl.BlockSpec((None, 1, sw), lambda g, s: (g, 0, 0)),
                  pl.BlockSpec((1, LANES), lambda g, s: (0, g))],
        out_specs=[pl.BlockSpec((nb, ts, LANES), lambda g, s: (0, s, g)),
                   pl.BlockSpec((None, nb, sw), lambda g, s: (g, 0, 0))],
        out_shape=[jax.ShapeDtypeStruct((nb, nl, d), F32),
                   jax.ShapeDtypeStruct((ng, nb, sw), F32)],
        scratch_shapes=[pltpu.VMEM((sw // LANES, nb * nj, LANES), F32),
                        pltpu.VMEM((sw // LANES, nb * nj, LANES), F32),
                        pltpu.VMEM((nb, sw), F32)],
        compiler_params=_cparams("parallel", "arbitrary"),
        name="s5_core",
    )(u, h0, w_state, w_intra, w_carry, a_pow, d_skip)


def _s5_state_in(re, im, ng):
    nb = re.shape[0]
    h = jnp.concatenate([re.reshape(nb, ng, -1), im.reshape(nb, ng, -1)], axis=-1)
    return h.transpose(1, 0, 2)


def _s5_state_out(h, g, p):
    ng, nb, sw = h.shape
    h = h.transpose(1, 0, 2)
    return h[..., :sw // 2].reshape(nb, g, p), h[..., sw // 2:].reshape(nb, g, p)


def _glu_out_kernel(g_ref, x_ref, wglu_ref, bglu_ref, wout_ref, gpost_ref, o_ref):
    g = g_ref[...]
    t = _dot(g.astype(BF16), wglu_ref[...]) + bglu_ref[...]
    z = (g * jax.nn.sigmoid(t)).astype(BF16)
    m = _dot(z, wout_ref[...])
    o_ref[...] = x_ref[...] + _rms(m, gpost_ref[...])


def _glu_out(g, x, w_glu, b_glu, w_out, g_post, tm):
    t, d = x.shape
    row = pl.BlockSpec((tm, d), lambda i: (i, 0))
    vec = pl.BlockSpec((1, d), lambda i: (0, 0))
    mat = pl.BlockSpec((d, d), lambda i: (0, 0))
    return pl.pallas_call(
        _glu_out_kernel,
        grid=(t // tm,),
        in_specs=[row, row, mat, vec, mat, vec],
        out_specs=row,
        out_shape=jax.ShapeDtypeStruct((t, d), F32),
        compiler_params=_cparams("parallel"),
        name="glu_out",
    )(g, x, w_glu, b_glu, w_out, g_post)


def _matmul_post_kernel(a1_ref, x1_ref, a2_ref, x2_ref, w_ref, gpost_ref, o_ref, *, n1):
    def emit(a_ref, x_ref):
        m = _dot(a_ref[...].astype(BF16), w_ref[...])
        o_ref[...] = x_ref[...] + _rms(m, gpost_ref[...])

    pl.when(pl.program_id(0) < n1)(lambda: emit(a1_ref, x1_ref))
    pl.when(pl.program_id(0) >= n1)(lambda: emit(a2_ref, x2_ref))


def _matmul_post_merged(a1, x1, a2, x2, w, g_post, tm):
    (t1, d), t2 = x1.shape, x2.shape[0]
    n1, n2 = t1 // tm, t2 // tm
    first = pl.BlockSpec((tm, d), lambda i: (jnp.minimum(i, n1 - 1), 0))
    second = pl.BlockSpec((tm, d), lambda i: (jnp.maximum(i - n1, 0), 0))
    return pl.pallas_call(
        functools.partial(_matmul_post_kernel, n1=n1),
        grid=(n1 + n2,),
        in_specs=[first, first, second, second,
                  pl.BlockSpec((d, d), lambda i: (0, 0)), pl.BlockSpec((1, d), lambda i: (0, 0))],
        out_specs=pl.BlockSpec((tm, d), lambda i: (i, 0)),
        out_shape=jax.ShapeDtypeStruct((t1 + t2, d), F32),
        compiler_params=_cparams("parallel"),
        name="matmul_post",
    )(a1, x1, a2, x2, w, g_post)


def _ffn_kernel(x_ref, gpre_ref, wg_ref, wu_ref, wd_ref, gpost_ref, o_ref, xn_scr, acc_scr):
    f, nf = pl.program_id(1), pl.num_programs(1)

    @pl.when(f == 0)
    def _():
        xn_scr[...] = _rms(x_ref[...], gpre_ref[...]).astype(BF16)
        acc_scr[...] = jnp.zeros_like(acc_scr)

    xn = xn_scr[...]
    h = (jax.nn.silu(_dot(xn, wg_ref[...])) * _dot(xn, wu_ref[...])).astype(BF16)
    acc_scr[...] += _dot(h, wd_ref[...])

    @pl.when(f == nf - 1)
    def _():
        o_ref[...] = x_ref[...] + _rms(acc_scr[...], gpost_ref[...])


def _ffn(x, g_pre, w_gate, w_up, w_down, g_post, tm, tf):
    t, d = x.shape
    ff = w_gate.shape[1]
    row = pl.BlockSpec((tm, d), lambda i, f: (i, 0))
    vec = pl.BlockSpec((1, d), lambda i, f: (0, 0))
    return pl.pallas_call(
        _ffn_kernel,
        grid=(t // tm, ff // tf),
        in_specs=[row, vec,
                  pl.BlockSpec((d, tf), lambda i, f: (0, f)),
                  pl.BlockSpec((d, tf), lambda i, f: (0, f)),
                  pl.BlockSpec((tf, d), lambda i, f: (f, 0)),
                  vec],
        out_specs=row,
        out_shape=jax.ShapeDtypeStruct((t, d), F32),
        scratch_shapes=[pltpu.VMEM((tm, d), BF16), pltpu.VMEM((tm, d), F32)],
        compiler_params=_cparams("parallel", "arbitrary"),
        name="ffn_dense",
    )(x, g_pre, w_gate, w_up, w_down, g_post)


META_G1, META_G2, META_E1, META_E2, META_R1, META_R2 = range(6)


def _router_kernel(x_ref, gpre_ref, wr_ref, br_ref, meta_ref, cnt_ref, cnt_scr):
    i = pl.program_id(0)

    @pl.when(i == 0)
    def _():
        cnt_scr[...] = jnp.zeros_like(cnt_scr)

    xn = _rms(x_ref[...], gpre_ref[...]).astype(BF16)
    logits = _dot(xn, wr_ref[...]) + br_ref[...]
    ex = jnp.exp(logits - jnp.max(logits, axis=-1, keepdims=True))
    probs = ex / jnp.sum(ex, axis=-1, keepdims=True)
    lane = lax.broadcasted_iota(jnp.int32, probs.shape, 1)
    p1 = jnp.max(probs, axis=-1, keepdims=True)
    i1 = jnp.min(jnp.where(probs == p1, lane, LANES), axis=-1, keepdims=True)
    rest = jnp.where(lane == i1, -1.0, probs)
    p2 = jnp.max(rest, axis=-1, keepdims=True)
    i2 = jnp.min(jnp.where(rest == p2, lane, LANES), axis=-1, keepdims=True)
    den = p1 + p2
    tm = probs.shape[0]
    chosen = (lane == i1) | (lane == i2)
    before = _tri01(tm, lambda r, c: c < r)
    seen = _dot(before, chosen.astype(BF16)) + cnt_scr[...]
    rank1 = jnp.sum(jnp.where(lane == i1, seen, 0.0), axis=-1, keepdims=True)
    rank2 = jnp.sum(jnp.where(lane == i2, seen, 0.0), axis=-1, keepdims=True)
    cnt_scr[...] += jnp.sum(chosen.astype(F32), axis=0, keepdims=True)
    cnt_ref[...] = cnt_scr[...]
    fields = {META_G1: p1 / den, META_G2: p2 / den, META_E1: i1.astype(F32), META_E2: i2.astype(F32),
              META_R1: rank1, META_R2: rank2}
    meta = jnp.zeros(probs.shape, F32)
    for k, v in fields.items():
        meta = jnp.where(lane == k, v, meta)
    meta_ref[...] = meta


def _router(x, g_pre, w_router, b_router, tm):
    t, d = x.shape
    ne = w_router.shape[1]
    wr = jnp.zeros((d, LANES), BF16).at[:, :ne].set(w_router.astype(BF16))
    br = jnp.full((1, LANES), -1e30, F32).at[0, :ne].set(b_router)
    return pl.pallas_call(
        _router_kernel,
        grid=(t // tm,),
        in_specs=[pl.BlockSpec((tm, d), lambda i: (i, 0)),
                  pl.BlockSpec((1, d), lambda i: (0, 0)),
                  pl.BlockSpec((d, LANES), lambda i: (0, 0)),
                  pl.BlockSpec((1, LANES), lambda i: (0, 0))],
        out_specs=[pl.BlockSpec((tm, LANES), lambda i: (i, 0)),
                   pl.BlockSpec((1, LANES), lambda i: (0, 0))],
        out_shape=[jax.ShapeDtypeStruct((t, LANES), F32), jax.ShapeDtypeStruct((1, LANES), F32)],
        scratch_shapes=[pltpu.VMEM((1, LANES), F32)],
        compiler_params=_cparams("arbitrary"),
        name="moe_router",
    )(x, g_pre, wr, br)


def _row_copy(src, src_row, dst, dst_row, sem):
    return pltpu.make_async_copy(src.at[pl.ds(src_row, 1)], dst.at[pl.ds(dst_row, 1)], sem)


def _dispatch_kernel(dst1_ref, dst2_ref, x_ref, zeros_hbm, xs_hbm, sem, *, tr):
    del zeros_hbm
    base = pl.program_id(0) * tr

    def each_row(fn):
        def body(r, _):
            t = base + r
            fn(_row_copy(x_ref, r, xs_hbm, dst1_ref[t], sem))
            fn(_row_copy(x_ref, r, xs_hbm, dst2_ref[t], sem))
            return 0

        lax.fori_loop(0, tr, body, 0, unroll=DMA_LOOP_UNROLL)

    each_row(lambda c: c.start())
    each_row(lambda c: c.wait())


def _dispatch(x, dst1, dst2, n_slots, tr):
    t, d = x.shape
    grid_spec = pltpu.PrefetchScalarGridSpec(
        num_scalar_prefetch=2,
        grid=(t // tr,),
        in_specs=[pl.BlockSpec((tr, d), lambda i, d1, d2: (i, 0)), pl.BlockSpec(memory_space=pl.ANY)],
        out_specs=pl.BlockSpec(memory_space=pl.ANY),
        scratch_shapes=[pltpu.SemaphoreType.DMA(())],
    )
    return pl.pallas_call(
        functools.partial(_dispatch_kernel, tr=tr),
        grid_spec=grid_spec,
        out_shape=jax.ShapeDtypeStruct((n_slots, d), F32),
        input_output_aliases={3: 0},
        compiler_params=_cparams("arbitrary"),
        name="moe_dispatch",
    )(dst1, dst2, x, jnp.zeros((n_slots, d), F32))


def _ffn_grouped_kernel(te_ref, tv_ref, x_ref, gpre_ref, wg_ref, wu_ref, wd_ref, y_ref, xn_scr):
    del te_ref
    i, f = pl.program_id(0), pl.program_id(1)
    valid = tv_ref[i] != 0

    @pl.when(f == 0)
    def _():
        xn_scr[...] = _rms(x_ref[...], gpre_ref[...]).astype(BF16)
        y_ref[...] = jnp.zeros_like(y_ref)

    @pl.when(valid)
    def _():
        xn = xn_scr[...]
        h = (jax.nn.silu(_dot(xn, wg_ref[...])) * _dot(xn, wu_ref[...])).astype(BF16)
        y_ref[...] += _dot(h, wd_ref[...])


def _ffn_grouped(xs, tile_expert, tile_valid, g_pre, w_gate, w_up, w_down, tm, tf):
    n_slots, d = xs.shape
    ff = w_gate.shape[2]
    row = pl.BlockSpec((tm, d), lambda i, f, te, tv: (i, 0))
    grid_spec = pltpu.PrefetchScalarGridSpec(
        num_scalar_prefetch=2,
        grid=(n_slots // tm, ff // tf),
        in_specs=[row,
                  pl.BlockSpec((1, d), lambda i, f, te, tv: (0, 0)),
                  pl.BlockSpec((None, d, tf), lambda i, f, te, tv: (te[i], 0, f)),
                  pl.BlockSpec((None, d, tf), lambda i, f, te, tv: (te[i], 0, f)),
                  pl.BlockSpec((None, tf, d), lambda i, f, te, tv: (te[i], f, 0))],
        out_specs=row,
        scratch_shapes=[pltpu.VMEM((tm, d), BF16)],
    )
    return pl.pallas_call(
        _ffn_grouped_kernel,
        grid_spec=grid_spec,
        out_shape=jax.ShapeDtypeStruct((n_slots, d), F32),
        compiler_params=_cparams("parallel", "arbitrary"),
        name="moe_ffn_grouped",
    )(tile_expert, tile_valid, xs, g_pre, w_gate, w_up, w_down)


def _combine_kernel(dst1_ref, dst2_ref, y_hbm, x_ref, meta_ref, gpost_ref, o_ref, ybuf, sems, *, tm, blk0):
    i, n = pl.program_id(0), pl.num_programs(0)

    def rows(step, slot, fn):
        base = (blk0 + step) * tm

        def body(r, _):
            t = base + r
            fn(_row_copy(y_hbm, dst1_ref[t], ybuf.at[slot, 0], r, sems.at[slot]))
            fn(_row_copy(y_hbm, dst2_ref[t], ybuf.at[slot, 1], r, sems.at[slot]))
            return 0

        lax.fori_loop(0, tm, body, 0, unroll=DMA_LOOP_UNROLL)

    @pl.when(i == 0)
    def _():
        rows(0, 0, lambda c: c.start())

    for slot in range(2):
        @pl.when((i + 1 < n) & ((i + 1) % 2 == slot))
        def _(slot=slot):
            rows(i + 1, slot, lambda c: c.start())

    for slot in range(2):
        @pl.when(i % 2 == slot)
        def _(slot=slot):
            rows(i, slot, lambda c: c.wait())
            meta = meta_ref[...]
            g1 = meta[:, META_G1:META_G1 + 1]
            g2 = meta[:, META_G2:META_G2 + 1]
            mix = g1 * ybuf[slot, 0] + g2 * ybuf[slot, 1]
            o_ref[...] = x_ref[...] + _rms(mix, gpost_ref[...])


def _combine(y, x, meta, dst1, dst2, g_post, tm, row0, n_rows):
    d = x.shape[1]
    blk0 = row0 // tm
    grid_spec = pltpu.PrefetchScalarGridSpec(
        num_scalar_prefetch=2,
        grid=(n_rows // tm,),
        in_specs=[pl.BlockSpec(memory_space=pl.ANY),
                  pl.BlockSpec((tm, d), lambda i, d1, d2: (blk0 + i, 0)),
                  pl.BlockSpec((tm, LANES), lambda i, d1, d2: (blk0 + i, 0)),
                  pl.BlockSpec((1, d), lambda i, d1, d2: (0, 0))],
        out_specs=pl.BlockSpec((tm, d), lambda i, d1, d2: (i, 0)),
        scratch_shapes=[pltpu.VMEM((2, 2, tm, d), F32), pltpu.SemaphoreType.DMA((2,))],
    )
    return pl.pallas_call(
        functools.partial(_combine_kernel, tm=tm, blk0=blk0),
        grid_spec=grid_spec,
        out_shape=jax.ShapeDtypeStruct((n_rows, d), F32),
        compiler_params=_cparams("arbitrary"),
        name="moe_combine",
    )(dst1, dst2, y, x, meta, g_post)


def _moe_plan(meta, counts, n_experts, tm):
    t = meta.shape[0]
    cnt = counts[0, :n_experts].astype(jnp.int32)
    padded = ((cnt + tm - 1) // tm) * tm
    ends = jnp.cumsum(padded)
    offs = ends - padded
    e1, e2 = meta[:, META_E1].astype(jnp.int32), meta[:, META_E2].astype(jnp.int32)
    dst1 = offs[e1] + meta[:, META_R1].astype(jnp.int32)
    dst2 = offs[e2] + meta[:, META_R2].astype(jnp.int32)
    n_tiles = (2 * t + n_experts * (tm - 1)) // tm
    starts = jnp.arange(n_tiles, dtype=jnp.int32) * tm
    tile_expert = jnp.minimum(jnp.sum(starts[:, None] >= ends[None, :], axis=1), n_experts - 1).astype(jnp.int32)
    tile_valid = (starts < ends[-1]).astype(jnp.int32)
    return dst1, dst2, tile_expert, tile_valid, n_tiles * tm


def _log_sigmoid(z):
    return jnp.minimum(z, 0.0) - jnp.log(1.0 + jnp.exp(-jnp.abs(z)))


def _fox_proj_prompt_kernel(x_ref, gpre_ref, wq_ref, wkvt_ref, wft_ref, bf_ref,
                            q_ref, kt_ref, vt_ref, lft_ref, *, scale):
    xn = _rms(x_ref[...], gpre_ref[...]).astype(BF16)
    q_ref[...] = (_dot(xn, wq_ref[...]) * scale).astype(BF16)
    d = kt_ref.shape[0]
    kvt = _dot_nt(wkvt_ref[...], xn)
    kt_ref[...] = kvt[:d]
    vt_ref[...] = kvt[d:]
    lft_ref[...] = _log_sigmoid(_dot_nt(wft_ref[...], xn) + bf_ref[...])


def _fox_proj_prompt(x, g_pre, wq, wkvt, wft, b_f, scale, tm):
    nb, nl, d = x.shape
    nh = wft.shape[0]
    const = lambda b, i: (0, 0)
    return pl.pallas_call(
        functools.partial(_fox_proj_prompt_kernel, scale=scale),
        grid=(nb, nl // tm),
        in_specs=[pl.BlockSpec((None, tm, d), lambda b, i: (b, i, 0)),
                  pl.BlockSpec((1, d), const),
                  pl.BlockSpec(wq.shape, const),
                  pl.BlockSpec(wkvt.shape, const),
                  pl.BlockSpec(wft.shape, const),
                  pl.BlockSpec((nh, 1), const)],
        out_specs=[pl.BlockSpec((None, tm, d), lambda b, i: (b, i, 0)),
                   pl.BlockSpec((None, d, tm), lambda b, i: (b, 0, i)),
                   pl.BlockSpec((None, d, tm), lambda b, i: (b, 0, i)),
                   pl.BlockSpec((None, nh, tm), lambda b, i: (b, 0, i))],
        out_shape=[jax.ShapeDtypeStruct((nb, nl, d), BF16),
                   jax.ShapeDtypeStruct((nb, d, nl), F32),
                   jax.ShapeDtypeStruct((nb, d, nl), F32),
                   jax.ShapeDtypeStruct((nb, nh, nl), F32)],
        compiler_params=_cparams("parallel", "parallel"),
        name="fox_proj_prompt",
    )(x, g_pre, wq, wkvt, wft, b_f)


def _tri01(n, rel):
    r = lax.broadcasted_iota(jnp.int32, (n, n), 0)
    c = lax.broadcasted_iota(jnp.int32, (n, n), 1)
    return rel(r, c).astype(BF16)


def _cumsum_lanes_kernel(x_ref, o_ref):
    upper = _tri01(LANES, lambda r, c: r <= c)
    carry = jnp.zeros((x_ref.shape[0], 1), F32)
    for blk in range(x_ref.shape[1] // LANES):
        sl = slice(blk * LANES, (blk + 1) * LANES)
        cum = _dot_exact_rhs01(x_ref[:, sl], upper) + carry
        o_ref[:, sl] = cum
        carry = cum[:, LANES - 1:LANES]


def _cumsum_lanes(x):
    nb, r, nl = x.shape
    spec = pl.BlockSpec((None, r, nl), lambda b: (b, 0, 0))
    return pl.pallas_call(
        _cumsum_lanes_kernel, grid=(nb,), in_specs=[spec], out_specs=spec,
        out_shape=jax.ShapeDtypeStruct(x.shape, F32),
        compiler_params=_cparams("parallel"), name="logf_cumsum",
    )(x)


def _prompt_attn_part(q_ref, kt_ref, vt_ref, c_ref, o_ref, hpair, qi, *, tq, hd):
    heads = LANES // hd
    half = tq // 2
    acc_w = hd + 2 * SUBLANES
    q0 = pl.multiple_of(qi * tq, tq)
    hslice = [slice(hh * hd, (hh + 1) * hd) for hh in range(heads)]
    qs = [q_ref[pl.ds(q0, tq), hs] for hs in hslice]

    def attend(hh, q_rows, m, acc, k0, width, causal):
        hs = hslice[hh]
        kt = kt_ref[hs, pl.ds(k0, width)].astype(BF16)
        vt = jnp.concatenate([vt_ref[hs, pl.ds(k0, width)].astype(BF16),
                              jnp.ones((acc_w - hd, width), BF16)], axis=0)
        cc = c_ref[pl.ds(hpair * heads + hh, 1), pl.ds(k0, width)]
        s = _dot(q_rows, kt) - cc
        if causal:
            row = lax.broadcasted_iota(jnp.int32, s.shape, 0)
            col = lax.broadcasted_iota(jnp.int32, s.shape, 1)
            s = jnp.where(row >= col, s, NEG_INF)
        blocks = [s[:, j * LANES:(j + 1) * LANES] for j in range(width // LANES)]
        m_new = jnp.maximum(m, jnp.max(functools.reduce(jnp.maximum, blocks), axis=-1, keepdims=True))
        alpha = jnp.exp(m - m_new)
        p = jnp.concatenate([jnp.exp(blk - m_new).astype(BF16) for blk in blocks], axis=1)
        return m_new, alpha[:, :acc_w] * acc + _dot_nt(p, vt)

    def full_tile(j, carry):
        k0 = pl.multiple_of(j * tq, tq)
        return tuple(attend(hh, qs[hh], *carry[hh], k0, tq, False) for hh in range(heads))

    init = tuple((jnp.full((tq, LANES), NEG_INF, F32), jnp.zeros((tq, acc_w), F32)) for _ in hslice)
    carry = lax.fori_loop(0, qi, full_tile, init)
    for hh, hs in enumerate(hslice):
        m, acc = attend(hh, qs[hh], *carry[hh], q0, half, True)
        _, acc_lo = attend(hh, qs[hh][half:], m[half:], acc[half:], pl.multiple_of(q0 + half, half), half, True)
        acc = jnp.concatenate([acc[:half], acc_lo], axis=0)
        o_ref[:, hs] = (acc[:, :hd] / acc[:, hd:hd + 1]).astype(o_ref.dtype)


def _fox_proj_sample_kernel(x_ref, gpre_ref, w_ref, wf_ref, bf_ref, q_ref, k_ref, v_ref, lf_ref, *, scale):
    xn = _rms(x_ref[...], gpre_ref[...]).astype(BF16)
    d = q_ref.shape[1]
    proj = _dot(xn, w_ref[...])
    q_ref[...] = proj[:, :d] * scale
    k_ref[...] = proj[:, d:2 * d]
    v_ref[...] = proj[:, 2 * d:]
    lf_ref[...] = _log_sigmoid(_dot(xn, wf_ref[...]) + bf_ref[...])


def _fox_proj_sample(x, g_pre, w_qkv, w_f, b_f, scale):
    t, d = x.shape
    nh = w_f.shape[1]
    wf = jnp.zeros((d, LANES), BF16).at[:, :nh].set(w_f)
    bf = jnp.zeros((1, LANES), F32).at[0, :nh].set(b_f)
    full = lambda a: pl.BlockSpec(a.shape, lambda i: (0,) * a.ndim)
    row = pl.BlockSpec((t, d), lambda i: (0, 0))
    return pl.pallas_call(
        functools.partial(_fox_proj_sample_kernel, scale=scale),
        grid=(1,),
        in_specs=[row, full(g_pre), full(w_qkv), full(wf), full(bf)],
        out_specs=[row, row, row, pl.BlockSpec((t, LANES), lambda i: (0, 0))],
        out_shape=[jax.ShapeDtypeStruct((t, d), F32)] * 3 + [jax.ShapeDtypeStruct((t, LANES), F32)],
        compiler_params=_cparams("arbitrary"),
        name="fox_proj_sample",
    )(x, g_pre, w_qkv, wf, bf)


def _paged_attn_part(q_ref, kc_refs, vc_refs, lfc_refs, kn_ref, vn_ref, lfn_ref, o_ref,
                     qbd, m_scr, l_scr, acc_scr, carry, bias_scr, p, n_chunks, *, ns, hd):
    gp = len(kc_refs)
    n_groups, g_rows, g_width = qbd.shape
    rows, d = n_groups * g_rows, n_groups * g_width
    nh = d // hd
    gmask = (lax.broadcasted_iota(jnp.int32, (g_rows, g_width), 0) // ns
             == lax.broadcasted_iota(jnp.int32, (g_rows, g_width), 1) // hd)
    expand = (lax.broadcasted_iota(jnp.int32, (rows, nh), 0) // ns
              == lax.broadcasted_iota(jnp.int32, (rows, nh), 1)).astype(BF16)
    g_cols = lambda g: slice(g * g_width, (g + 1) * g_width)
    g_rws = lambda g: slice(g * g_rows, (g + 1) * g_rows)

    @pl.when(p == 0)
    def _():
        for g in range(n_groups):
            qt = jnp.concatenate([q_ref[:, g_cols(g)]] * (g_rows // ns), axis=0)
            qbd[g] = jnp.where(gmask, qt, 0.0).astype(BF16)
        m_scr[...] = jnp.full_like(m_scr, NEG_INF)
        l_scr[...] = jnp.zeros_like(l_scr)
        acc_scr[...] = jnp.zeros_like(acc_scr)
        carry[...] = jnp.zeros_like(carry)

    def per_group(fn):
        return jnp.concatenate([fn(g) for g in range(n_groups)], axis=0)

    def update(s, pv_of_group):
        m_new = jnp.maximum(m_scr[...], jnp.max(s, axis=-1, keepdims=True))
        alpha = jnp.exp(m_scr[...] - m_new)
        pr = jnp.exp(s - m_new)
        l_scr[...] = alpha * l_scr[...] + jnp.sum(pr, axis=-1, keepdims=True)
        pr = pr.astype(BF16)
        acc_scr[...] = alpha * acc_scr[...] + per_group(lambda g: pv_of_group(g, pr[g_rws(g)]))
        m_scr[...] = m_new

    lfs = [r[...] for r in lfc_refs]
    after = _tri01(LANES, lambda r, c: r > c)
    within = _dot_exact_rhs01(jnp.concatenate(lfs, axis=0), after)
    run = carry[...]
    for k in reversed(range(gp)):
        w_k = within[k * nh:(k + 1) * nh]
        bias_scr[:, k * LANES:(k + 1) * LANES] = run + w_k
        run = run + (w_k[:, 0:1] + lfs[k][:, 0:1])
    carry[...] = run
    def pages(refs, g):
        return jnp.concatenate([r[g_cols(g), :].astype(BF16) for r in refs], axis=1)

    s = per_group(lambda g: _dot(qbd[g], pages(kc_refs, g)))
    s = jnp.concatenate([s[h * ns:(h + 1) * ns] + bias_scr[h:h + 1, :] for h in range(nh)], axis=0)
    update(s, lambda g, pr: _dot_nt(pr, pages(vc_refs, g)))

    @pl.when(p == n_chunks - 1)
    def _():
        upto = _tri01(LANES, lambda r, c: r <= c)
        cn = _dot_exact_rhs01(lfn_ref[...], upto)
        pad = jnp.zeros((LANES - ns, d), F32)
        kn = jnp.concatenate([kn_ref[...], pad], axis=0).astype(BF16)
        vn = jnp.concatenate([vn_ref[...], pad], axis=0).astype(BF16)
        sn = per_group(lambda g: _dot_nt(qbd[g], kn[:, g_cols(g)])) - _dot_exact_lhs01(expand, cn)
        qpos = lax.broadcasted_iota(jnp.int32, (rows, LANES), 0) % ns
        kpos = lax.broadcasted_iota(jnp.int32, (rows, LANES), 1)
        update(jnp.where(kpos <= qpos, sn, NEG_INF), lambda g, pr: _dot(pr, vn[:, g_cols(g)]))
        o = acc_scr[...] / l_scr[...]
        for g in range(n_groups):
            og = jnp.where(gmask, o[g_rws(g)], 0.0)
            o_ref[:, g_cols(g)] = functools.reduce(
                jnp.add, [og[h * ns:(h + 1) * ns] for h in range(g_rows // ns)])


def _fox_attn_kernel(tbl_ref, *refs, gp, n_chunks, n_sample_steps, n_hpairs, n_qtiles, n_prompt_steps,
                     ns, hd, tq):
    del tbl_ref
    sq_ref, refs = refs[0], refs[1:]
    kc_refs, vc_refs, lfc_refs = refs[:gp], refs[gp:2 * gp], refs[2 * gp:3 * gp]
    kn_ref, vn_ref, lfn_ref, q_ref, kt_ref, vt_ref, c_ref, os_ref, op_ref = refs[3 * gp:3 * gp + 9]
    decode_scratch = refs[3 * gp + 9:]
    s = pl.program_id(0)

    @pl.when(s < n_sample_steps)
    def _():
        _paged_attn_part(sq_ref, kc_refs, vc_refs, lfc_refs, kn_ref, vn_ref, lfn_ref, os_ref, *decode_scratch,
                         s % n_chunks, n_chunks, ns=ns, hd=hd)

    @pl.when(s < n_prompt_steps)
    def _():
        _prompt_attn_part(q_ref, kt_ref, vt_ref, c_ref, op_ref, (s // n_qtiles) % n_hpairs, s % n_qtiles,
                          tq=tq, hd=hd)


def _fox_attn(page_table, sq, kc, vc, lfc, kn, vn, lfn, q, kt, vt, c, hd):
    db, ns, d = sq.shape
    n_pages = page_table.shape[1]
    nh = d // hd
    rows = nh * ns
    page = kc.shape[2]
    assert rows == LANES and page == LANES
    gp = next(g for g in (16, 8, 4, 2, 1) if n_pages % g == 0)
    n_chunks = n_pages // gp
    n_sample_steps = db * n_chunks

    nb, nl, _ = q.shape
    tq = min(nl, 512)
    assert nl % tq == 0 and tq % (2 * LANES) == 0
    n_hpairs, n_qtiles = d // LANES, nl // tq
    n_prompt_steps = nb * n_hpairs * n_qtiles

    def sample_pos(s):
        s = jnp.minimum(s, n_sample_steps - 1)
        return s // n_chunks, s % n_chunks

    def prompt_pos(s):
        s = jnp.minimum(s, n_prompt_steps - 1)
        return s // (n_hpairs * n_qtiles), (s // n_qtiles) % n_hpairs, s % n_qtiles

    seq = pl.BlockSpec((None, ns, d), lambda s, tbl: (sample_pos(s)[0], 0, 0))

    def pg(k):
        def index(s, tbl):
            b, p = sample_pos(s)
            return tbl[b, (n_chunks - 1 - p) * gp + k], 0, 0
        return index

    def pidx(fn):
        return lambda s, tbl: fn(*prompt_pos(s))

    grid_spec = pltpu.PrefetchScalarGridSpec(
        num_scalar_prefetch=1,
        grid=(max(n_sample_steps, n_prompt_steps),),
        in_specs=([seq]
                  + [pl.BlockSpec((None, d, page), pg(k)) for k in range(gp)]
                  + [pl.BlockSpec((None, d, page), pg(k)) for k in range(gp)]
                  + [pl.BlockSpec((None, nh, page), pg(k)) for k in range(gp)]
                  + [seq, seq, pl.BlockSpec((None, nh, LANES), lambda s, tbl: (sample_pos(s)[0], 0, 0))]
                  + [pl.BlockSpec((None, nl, LANES), pidx(lambda b, h, i: (b, 0, h))),
                     pl.BlockSpec((None, LANES, nl), pidx(lambda b, h, i: (b, h, 0))),
                     pl.BlockSpec((None, LANES, nl), pidx(lambda b, h, i: (b, h, 0))),
                     pl.BlockSpec((None, nh, nl), pidx(lambda b, h, i: (b, 0, 0)))]),
        out_specs=[seq, pl.BlockSpec((None, tq, LANES), pidx(lambda b, h, i: (b, i, h)))],
        scratch_shapes=[pltpu.VMEM((d // V7X_MXU_DIM, rows * V7X_MXU_DIM // d, V7X_MXU_DIM), BF16),
                        pltpu.VMEM((rows, 1), F32),
                        pltpu.VMEM((rows, 1), F32),
                        pltpu.VMEM((rows, V7X_MXU_DIM), F32),
                        pltpu.VMEM((nh, 1), F32),
                        pltpu.VMEM((nh, gp * page), F32)],
    )
    return pl.pallas_call(
        functools.partial(_fox_attn_kernel, gp=gp, n_chunks=n_chunks, n_sample_steps=n_sample_steps,
                          n_hpairs=n_hpairs, n_qtiles=n_qtiles, n_prompt_steps=n_prompt_steps,
                          ns=ns, hd=hd, tq=tq),
        grid_spec=grid_spec,
        out_shape=[jax.ShapeDtypeStruct((db, ns, d), F32), jax.ShapeDtypeStruct((nb, nl, d), BF16)],
        compiler_params=_cparams("arbitrary"),
        name="fox_attn",
    )(page_table, sq, *([kc] * gp), *([vc] * gp), *([lfc] * gp), kn, vn, lfn, q, kt, vt, c)


def _div_tile(t, candidates):
    return next((tm for tm in candidates if t % tm == 0), t)


def _row_tile(t):
    return _div_tile(t, (1024, 512, 256, 128))


def _ff_tile(ff):
    return _div_tile(ff, (7 * V7X_MXU_DIM, 4 * V7X_MXU_DIM, 2 * V7X_MXU_DIM, V7X_MXU_DIM, LANES))


def kernel(x_prompt, x_sample, state_ssm_re, state_ssm_im, cache_k, cache_v, cache_logf, page_table, norm_mix_pre, norm_mix_post, norm_ffn_pre, norm_ffn_post, ssm_w_in, ssm_lambda_re, ssm_lambda_im, ssm_log_dt, ssm_b_re, ssm_b_im, ssm_c_re, ssm_c_im, ssm_d, ssm_w_glu, ssm_b_glu, ssm_w_out, fox_w_in, fox_b_f, fox_w_out, ffn_w_gate, ffn_w_up, ffn_w_down, moe_w_router, moe_b_router, moe_w_gate, moe_w_up, moe_w_down):
    nb, nl, d = x_prompt.shape
    db, ns, _ = x_sample.shape
    n_groups, n_state = ssm_lambda_re.shape[1:]
    nh = fox_b_f.shape[1]
    hd = d // nh
    scale = hd ** -0.5
    ng = d // LANES
    streams = [x_prompt.reshape(nb * nl, d), x_sample.reshape(db * ns, d)]
    dims = [(nb, nl), (db, ns)]
    tms = [_row_tile(x.shape[0]) for x in streams]
    vec = lambda a: a.reshape(1, -1)

    li = 0
    w_in = ssm_w_in[li].astype(BF16)
    w_glu = ssm_w_glu[li].astype(BF16)
    w_out = ssm_w_out[li].astype(BF16)
    s5w = _s5_prepare(ssm_lambda_re[li], ssm_lambda_im[li], ssm_log_dt[li], ssm_b_re[li], ssm_b_im[li],
                      ssm_c_re[li], ssm_c_im[li])
    wg, wu, wd = (w[li].astype(BF16) for w in (ffn_w_gate, ffn_w_up, ffn_w_down))
    h0s = [jnp.zeros((ng, nb, 2 * (n_groups // ng) * n_state), F32),
           _s5_state_in(state_ssm_re[li], state_ssm_im[li], ng)]
    ssm_states = []
    for si in range(2):
        x, (b_, l_), tm = streams[si], dims[si], tms[si]
        u = _norm_matmul(x, vec(norm_mix_pre[0]), w_in, tm)
        g_act, h_fin = _s5_core(u.reshape(b_, l_, d), h0s[si], *s5w, vec(ssm_d[li]))
        ssm_states.append(_s5_state_out(h_fin, n_groups, n_state))
        x = _glu_out(g_act.reshape(b_ * l_, d), x, w_glu, vec(ssm_b_glu[li]), w_out, vec(norm_mix_post[0]), tm)
        streams[si] = _ffn(x, vec(norm_ffn_pre[0]), wg, wu, wd, vec(norm_ffn_post[0]), min(tm, FFN_ROW_TILE),
                           _ff_tile(wg.shape[1]))

    w_fox = fox_w_in[li]
    w_fox_t = w_fox.T
    wq = w_fox[:, :d].astype(BF16)
    wkvt = w_fox_t[d:3 * d].astype(BF16)
    wft = w_fox_t[3 * d:].astype(BF16)
    w_o = fox_w_out[li].astype(BF16)

    xp = streams[0]
    q, kt, vt, lft = _fox_proj_prompt(xp.reshape(nb, nl, d), vec(norm_mix_pre[1]), wq, wkvt, wft,
                                      fox_b_f[li].reshape(nh, 1), scale, min(nl, 1024))
    c = _cumsum_lanes(lft)
    tp, tsm = nb * nl, db * ns
    k_prompt = kt.reshape(nb, nh, hd, nl).transpose(0, 3, 1, 2)[None]
    v_prompt = vt.reshape(nb, nh, hd, nl).transpose(0, 3, 1, 2)[None]
    logf_prompt = lft.transpose(0, 2, 1)[None]

    xs = streams[1]
    qs, ks, vs, lfs = _fox_proj_sample(xs, vec(norm_mix_pre[1]), w_fox[:, :3 * d].astype(BF16),
                                       w_fox[:, 3 * d:].astype(BF16), fox_b_f[li], scale)
    n_pool, page = cache_k.shape[1], cache_k.shape[2]
    kc = cache_k[li].transpose(0, 2, 3, 1).reshape(n_pool, d, page)
    vc = cache_v[li].transpose(0, 2, 3, 1).reshape(n_pool, d, page)
    lfc = cache_logf[li].transpose(0, 2, 1)
    lfs = lfs[:, :nh]
    lfn = jnp.zeros((db, nh, LANES), F32).at[:, :, :ns].set(lfs.reshape(db, ns, nh).transpose(0, 2, 1))
    os_, o = _fox_attn(page_table, qs.reshape(db, ns, d), kc, vc, lfc,
                       ks.reshape(db, ns, d), vs.reshape(db, ns, d), lfn, q, kt, vt, c, hd)
    small = (256, 128, 64, 32, 16, 8)
    x_all = _matmul_post_merged(o.reshape(tp, d), xp, os_.reshape(tsm, d), xs, w_o, vec(norm_mix_post[1]),
                                _div_tile(tsm, small))
    k_sample = ks.reshape(1, db, ns, nh, hd)
    v_sample = vs.reshape(1, db, ns, nh, hd)
    logf_sample = lfs.reshape(1, db, ns, nh)

    mg, mu, md = (w[li].astype(BF16) for w in (moe_w_gate, moe_w_up, moe_w_down))
    n_experts = mg.shape[0]
    t_route = _div_tile(tp + tsm, small)
    meta, counts = _router(x_all, vec(norm_ffn_pre[1]), moe_w_router[li], moe_b_router[li], t_route)
    dst1, dst2, tile_expert, tile_valid, n_slots = _moe_plan(meta, counts, n_experts, FFN_ROW_TILE)
    x_sorted = _dispatch(x_all, dst1, dst2, n_slots, t_route)
    y_sorted = _ffn_grouped(x_sorted, tile_expert, tile_valid, vec(norm_ffn_pre[1]), mg, mu, md, FFN_ROW_TILE,
                            _ff_tile(mg.shape[2]))
    outs = [_combine(y_sorted, x_all, meta, dst1, dst2, vec(norm_ffn_post[1]), _div_tile(n, small), r0, n)
            for r0, n in ((0, tp), (tp, tsm))]

    (re_p, im_p), (re_s, im_s) = ssm_states
    return (outs[0].reshape(nb, nl, d), outs[1].reshape(db, ns, d),
            re_p[None], im_p[None], re_s[None], im_s[None],
            k_prompt, v_prompt, logf_prompt, k_sample, v_sample, logf_sample)
```

```python
import functools

import jax
import jax.numpy as jnp
from jax import lax
from jax.experimental import pallas as pl
from jax.experimental.pallas import tpu as pltpu

F32 = jnp.float32
BF16 = jnp.bfloat16
RMS_EPS = 1e-6
LANES = 128
SUBLANES = 8
V7X_MXU_DIM = 256
V7X_VMEM_LIMIT_BYTES = 56 << 20
S5_CHUNK = 8
S5_SCAN_UNROLL = 8
FFN_ROW_TILE = 512
DMA_LOOP_UNROLL = 8
NEG_INF = float("-inf")


def _cparams(*sem):
    return pltpu.CompilerParams(dimension_semantics=sem, vmem_limit_bytes=V7X_VMEM_LIMIT_BYTES)


def _rms(x, g):
    return x * lax.rsqrt(jnp.mean(x * x, axis=-1, keepdims=True) + RMS_EPS) * g


def _dot(a, b):
    return jnp.dot(a, b, preferred_element_type=F32)


def _dot_nt(a, b):
    return lax.dot_general(a, b, (((1,), (1,)), ((), ())), preferred_element_type=F32)


def _split3(x):
    hi = x.astype(BF16)
    r = x - hi.astype(F32)
    mid = r.astype(BF16)
    lo = (r - mid.astype(F32)).astype(BF16)
    return hi, mid, lo


def _dot_exact_rhs01(x, m01):
    hi, mid, lo = _split3(x)
    return _dot(hi, m01) + _dot(mid, m01) + _dot(lo, m01)


def _dot_exact_lhs01(m01, x):
    hi, mid, lo = _split3(x)
    return _dot(m01, hi) + _dot(m01, mid) + _dot(m01, lo)


def _norm_matmul_kernel(x_ref, g_ref, w_ref, o_ref):
    xn = _rms(x_ref[...], g_ref[...]).astype(BF16)
    o_ref[...] = _dot(xn, w_ref[...])


def _norm_matmul(x, g, w, tm):
    t, d = x.shape
    n = w.shape[1]
    return pl.pallas_call(
        _norm_matmul_kernel,
        grid=(t // tm,),
        in_specs=[pl.BlockSpec((tm, d), lambda i: (i, 0)),
                  pl.BlockSpec((1, d), lambda i: (0, 0)),
                  pl.BlockSpec((d, n), lambda i: (0, 0))],
        out_specs=pl.BlockSpec((tm, n), lambda i: (i, 0)),
        out_shape=jax.ShapeDtypeStruct((t, n), F32),
        compiler_params=_cparams("parallel"),
        name="norm_matmul",
    )(x, g, w)


def _s5_prepare(lam_re, lam_im, log_dt, b_re, b_im, c_re, c_im):
    tc = S5_CHUNK
    g, p = lam_re.shape
    c = b_re.shape[-1]
    ng = (g * c) // LANES
    gl = g // ng
    hp = lax.Precision.HIGHEST
    dt = jnp.exp(log_dt)[:, None]
    mag = jnp.exp(lam_re * dt)
    ar = mag * jnp.cos(lam_im * dt)
    ai = mag * jnp.sin(lam_im * dt)
    den = lam_re * lam_re + lam_im * lam_im
    nr = ar - 1.0
    coef_r = (nr * lam_re + ai * lam_im) / den
    coef_i = (ai * lam_re - nr * lam_im) / den
    bbr = coef_r[..., None] * b_re - coef_i[..., None] * b_im
    bbi = coef_r[..., None] * b_im + coef_i[..., None] * b_re
    prs, pis = [jnp.ones_like(ar)], [jnp.zeros_like(ar)]
    for _ in range(tc):
        nr_, ni_ = prs[-1] * ar - pis[-1] * ai, prs[-1] * ai + pis[-1] * ar
        prs.append(nr_)
        pis.append(ni_)
    pr = jnp.stack(prs)
    pi = jnp.stack(pis)
    prs_ = pr[tc - 1 - jnp.arange(tc)][..., None]
    pis_ = pi[tc - 1 - jnp.arange(tc)][..., None]
    ms = jnp.stack([prs_ * bbr - pis_ * bbi, prs_ * bbi + pis_ * bbr])
    ms = ms.transpose(2, 1, 4, 0, 3).reshape(ng, gl, tc, c, 2 * p)
    m_state = ms.transpose(0, 2, 1, 3, 4).reshape(ng, tc * gl * c, 2 * p)

    er = pr[:tc, :, :, None] * bbr - pi[:tc, :, :, None] * bbi
    ei = pr[:tc, :, :, None] * bbi + pi[:tc, :, :, None] * bbr
    kk = (jnp.einsum("gdp,kgpc->kgdc", c_re, er, precision=hp)
          - jnp.einsum("gdp,kgpc->kgdc", c_im, ei, precision=hp))
    lag = jnp.arange(tc)[None, :] - jnp.arange(tc)[:, None]
    kt = jnp.where((lag >= 0)[:, :, None, None, None], kk[jnp.clip(lag, 0)], 0.0)
    kt = kt.transpose(2, 0, 4, 1, 3).reshape(ng, gl, tc, c, tc * c)
    m_intra = kt.transpose(0, 2, 1, 3, 4).reshape(ng, tc * gl * c, tc * c)

    pr1 = pr[1:].transpose(1, 2, 0)[:, :, :, None]
    pi1 = pi[1:].transpose(1, 2, 0)[:, :, :, None]
    cre = c_re.transpose(0, 2, 1)[:, :, None, :]
    cim = c_im.transpose(0, 2, 1)[:, :, None, :]
    hh = jnp.stack([cre * pr1 - cim * pi1, -(cre * pi1 + cim * pr1)])
    m_carry = hh.reshape(2, ng, gl * p, tc * c).transpose(1, 0, 2, 3).reshape(ng, 2 * gl * p, tc * c)

    w_state = _s5_expand(m_state, gl, row_group=c, col_block=gl * p, col_sub=p)
    w_intra = _s5_expand(m_intra, gl, row_group=c, col_block=gl * c, col_sub=c)
    w_carry = _s5_expand(m_carry, gl, row_group=p, col_block=gl * c, col_sub=c)
    a_pow = jnp.concatenate([pr[tc].reshape(ng, 1, gl * p), pi[tc].reshape(ng, 1, gl * p)], axis=-1)
    return w_state, w_intra, w_carry, a_pow


def _s5_expand_kernel(m_ref, o_ref, *, gl, row_group, col_block, col_sub):
    rows, cols = o_ref.shape
    s_i = lax.broadcasted_iota(jnp.int32, (m_ref.shape[1], cols), 0)
    j_i = lax.broadcasted_iota(jnp.int32, (m_ref.shape[1], cols), 1)
    place = (s_i == (j_i // col_block) * col_sub + j_i % col_sub).astype(BF16)
    spread = _dot(m_ref[...].astype(BF16), place)
    r_g = (lax.broadcasted_iota(jnp.int32, (rows, cols), 0) // row_group) % gl
    c_g = (lax.broadcasted_iota(jnp.int32, (rows, cols), 1) % col_block) // col_sub
    o_ref[...] = jnp.where(r_g == c_g, spread, 0.0).astype(o_ref.dtype)


def _s5_expand(m, gl, row_group, col_block, col_sub):
    ng, rows, k = m.shape
    cols = k * gl
    return pl.pallas_call(
        functools.partial(_s5_expand_kernel, gl=gl, row_group=row_group, col_block=col_block, col_sub=col_sub),
        grid=(ng,),
        in_specs=[pl.BlockSpec((None, rows, k), lambda a: (a, 0, 0))],
        out_specs=pl.BlockSpec((None, rows, cols), lambda a: (a, 0, 0)),
        out_shape=jax.ShapeDtypeStruct((ng, rows, cols), BF16),
        compiler_params=_cparams("parallel"),
        name="s5_expand",
    )(m)


def _s5_core_kernel(u_ref, h0_ref, ws_ref, wi_ref, wh_ref, ap_ref, d_ref, g_ref, hout_ref,
                    s_scr, hs_scr, hcar, *, nb, nj):
    tc = S5_CHUNK
    rows = nb * nj
    nk = s_scr.shape[0]
    half = nk // 2
    lane_blk = lambda k: slice(k * LANES, (k + 1) * LANES)

    @pl.when(pl.program_id(1) == 0)
    def _():
        hcar[...] = h0_ref[...]

    def u_at(t):
        if nj == 1:
            return u_ref[:, t, :]
        return u_ref[:, pl.ds(t, nj, stride=tc), :].reshape(rows, LANES)

    us = [u_at(t) for t in range(tc)]
    x = jnp.concatenate([v.astype(BF16) for v in us], axis=1)
    s_all = _dot(x, ws_ref[...])
    for k in range(nk):
        s_scr[k] = s_all[:, lane_blk(k)]

    a_pow = [ap_ref[:, lane_blk(k)] for k in range(nk)]

    def rows_at(j):
        return slice(None) if nj == 1 else pl.ds(j, nb, stride=nj)

    def step(j, h):
        for k in range(nk):
            hs_scr[k, rows_at(j), :] = h[k]
        new_re, new_im = [], []
        for k in range(half):
            hr, hi, ar, ai = h[k], h[k + half], a_pow[k], a_pow[k + half]
            new_re.append(ar * hr - ai * hi + s_scr[k, rows_at(j), :])
            new_im.append(ar * hi + ai * hr + s_scr[k + half, rows_at(j), :])
        return tuple(new_re + new_im)

    h = tuple(hcar[:, lane_blk(k)] for k in range(nk))
    h = step(0, h) if nj == 1 else lax.fori_loop(0, nj, step, h, unroll=S5_SCAN_UNROLL)
    for k in range(nk):
        hcar[:, lane_blk(k)] = h[k]
        hout_ref[:, lane_blk(k)] = h[k]

    hs = jnp.concatenate([hs_scr[k].astype(BF16) for k in range(nk)], axis=1)
    y = _dot(x, wi_ref[...]) + _dot(hs, wh_ref[...])
    d = d_ref[...]
    for t in range(tc):
        gt = jax.nn.gelu(y[:, lane_blk(t)] + d * us[t])
        if nj == 1:
            g_ref[:, t, :] = gt
        else:
            g_ref[:, pl.ds(t, nj, stride=tc), :] = gt.reshape(nb, nj, LANES)


def _s5_core(u, h0, w_state, w_intra, w_carry, a_pow, d_skip):
    nb, nl, d = u.shape
    ng = d // LANES
    ts = min(nl, 512)
    nj = ts // S5_CHUNK
    sw = w_state.shape[2]
    kern = functools.partial(_s5_core_kernel, nb=nb, nj=nj)
    return pl.pallas_call(
        kern,
        grid=(ng, nl // ts),
        in_specs=[pl.BlockSpec((nb, ts, LANES), lambda g, s: (0, s, g)),
                  pl.BlockSpec((None, nb, sw), lambda g, s: (g, 0, 0)),
                  pl.BlockSpec((None,) + w_state.shape[1:], lambda g, s: (g, 0, 0)),
                  pl.BlockSpec((None,) + w_intra.shape[1:], lambda g, s: (g, 0, 0)),
                  pl.BlockSpec((None,) + w_carry.shape[1:], lambda g, s: (g, 0, 0)),
                  pl.BlockSpec((None, 1, sw), lambda g, s: (g, 0, 0)),
                  pl.BlockSpec((1, LANES), lambda g, s: (0, g))],
        out_specs=[pl.BlockSpec((nb, ts, LANES), lambda g, s: (0, s, g)),
                   pl.BlockSpec((None, nb, sw), lambda g, s: (g, 0, 0))],
        out_shape=[jax.ShapeDtypeStruct((nb, nl, d), F32),
                   jax.ShapeDtypeStruct((ng, nb, sw), F32)],
        scratch_shapes=[pltpu.VMEM((sw // LANES, nb * nj, LANES), F32),
                        pltpu.VMEM((sw // LANES, nb * nj, LANES), F32),
                        pltpu.VMEM((nb, sw), F32)],
        compiler_params=_cparams("parallel", "arbitrary"),
        name="s5_core",
    )(u, h0, w_state, w_intra, w_carry, a_pow, d_skip)


def _s5_state_in(re, im, ng):
    nb = re.shape[0]
    h = jnp.concatenate([re.reshape(nb, ng, -1), im.reshape(nb, ng, -1)], axis=-1)
    return h.transpose(1, 0, 2)


def _s5_state_out(h, g, p):
    ng, nb, sw = h.shape
    h = h.transpose(1, 0, 2)
    return h[..., :sw // 2].reshape(nb, g, p), h[..., sw // 2:].reshape(nb, g, p)


def _glu_out_kernel(g_ref, x_ref, wglu_ref, bglu_ref, wout_ref, gpost_ref, o_ref):
    g = g_ref[...]
    t = _dot(g.astype(BF16), wglu_ref[...]) + bglu_ref[...]
    z = (g * jax.nn.sigmoid(t)).astype(BF16)
    m = _dot(z, wout_ref[...])
    o_ref[...] = x_ref[...] + _rms(m, gpost_ref[...])


def _glu_out(g, x, w_glu, b_glu, w_out, g_post, tm):
    t, d = x.shape
    row = pl.BlockSpec((tm, d), lambda i: (i, 0))
    vec = pl.BlockSpec((1, d), lambda i: (0, 0))
    mat = pl.BlockSpec((d, d), lambda i: (0, 0))
    return pl.pallas_call(
        _glu_out_kernel,
        grid=(t // tm,),
        in_specs=[row, row, mat, vec, mat, vec],
        out_specs=row,
        out_shape=jax.ShapeDtypeStruct((t, d), F32),
        compiler_params=_cparams("parallel"),
        name="glu_out",
    )(g, x, w_glu, b_glu, w_out, g_post)


def _matmul_post_route_kernel(a1_ref, x1_ref, a2_ref, x2_ref, w_ref, gpost_ref, gpre_ref, wr_ref, br_ref,
                              o_ref, meta_ref, cnt_ref, cnt_scr, *, n1):
    i = pl.program_id(0)

    @pl.when(i == 0)
    def _():
        cnt_scr[...] = jnp.zeros_like(cnt_scr)

    def emit(a_ref, x_ref):
        m = _dot(a_ref[...].astype(BF16), w_ref[...])
        x_new = x_ref[...] + _rms(m, gpost_ref[...])
        o_ref[...] = x_new
        meta_ref[...] = _route(x_new, gpre_ref, wr_ref, br_ref, cnt_scr)
        cnt_ref[...] = cnt_scr[...]

    pl.when(i < n1)(lambda: emit(a1_ref, x1_ref))
    pl.when(i >= n1)(lambda: emit(a2_ref, x2_ref))


def _matmul_post_route(a1, x1, a2, x2, w, g_post, g_pre_next, wr, br, tm):
    (t1, d), t2 = x1.shape, x2.shape[0]
    n1, n2 = t1 // tm, t2 // tm
    first = pl.BlockSpec((tm, d), lambda i: (jnp.minimum(i, n1 - 1), 0))
    second = pl.BlockSpec((tm, d), lambda i: (jnp.maximum(i - n1, 0), 0))
    vec = pl.BlockSpec((1, d), lambda i: (0, 0))
    lane_vec = pl.BlockSpec((1, LANES), lambda i: (0, 0))
    return pl.pallas_call(
        functools.partial(_matmul_post_route_kernel, n1=n1),
        grid=(n1 + n2,),
        in_specs=[first, first, second, second, pl.BlockSpec((d, d), lambda i: (0, 0)), vec,
                  vec, pl.BlockSpec((d, LANES), lambda i: (0, 0)), lane_vec],
        out_specs=[pl.BlockSpec((tm, d), lambda i: (i, 0)), pl.BlockSpec((tm, LANES), lambda i: (i, 0)), lane_vec],
        out_shape=[jax.ShapeDtypeStruct((t1 + t2, d), F32), jax.ShapeDtypeStruct((t1 + t2, LANES), F32),
                   jax.ShapeDtypeStruct((1, LANES), F32)],
        scratch_shapes=[pltpu.VMEM((1, LANES), F32)],
        compiler_params=_cparams("arbitrary"),
        name="matmul_post_route",
    )(a1, x1, a2, x2, w, g_post, g_pre_next, wr, br)


def _ffn_kernel(x_ref, gpre_ref, wg_ref, wu_ref, wd_ref, gpost_ref, o_ref, xn_scr, acc_scr):
    f, nf = pl.program_id(1), pl.num_programs(1)

    @pl.when(f == 0)
    def _():
        xn_scr[...] = _rms(x_ref[...], gpre_ref[...]).astype(BF16)
        acc_scr[...] = jnp.zeros_like(acc_scr)

    xn = xn_scr[...]
    h = (jax.nn.silu(_dot(xn, wg_ref[...])) * _dot(xn, wu_ref[...])).astype(BF16)
    acc_scr[...] += _dot(h, wd_ref[...])

    @pl.when(f == nf - 1)
    def _():
        o_ref[...] = x_ref[...] + _rms(acc_scr[...], gpost_ref[...])


def _ffn(x, g_pre, w_gate, w_up, w_down, g_post, tm, tf):
    t, d = x.shape
    ff = w_gate.shape[1]
    row = pl.BlockSpec((tm, d), lambda i, f: (i, 0))
    vec = pl.BlockSpec((1, d), lambda i, f: (0, 0))
    return pl.pallas_call(
        _ffn_kernel,
        grid=(t // tm, ff // tf),
        in_specs=[row, vec,
                  pl.BlockSpec((d, tf), lambda i, f: (0, f)),
                  pl.BlockSpec((d, tf), lambda i, f: (0, f)),
                  pl.BlockSpec((tf, d), lambda i, f: (f, 0)),
                  vec],
        out_specs=row,
        out_shape=jax.ShapeDtypeStruct((t, d), F32),
        scratch_shapes=[pltpu.VMEM((tm, d), BF16), pltpu.VMEM((tm, d), F32)],
        compiler_params=_cparams("parallel", "arbitrary"),
        name="ffn_dense",
    )(x, g_pre, w_gate, w_up, w_down, g_post)


META_G1, META_G2, META_E1, META_E2, META_R1, META_R2 = range(6)


def _route(x, gpre_ref, wr_ref, br_ref, cnt_scr):
    xn = _rms(x, gpre_ref[...]).astype(BF16)
    logits = _dot(xn, wr_ref[...]) + br_ref[...]
    ex = jnp.exp(logits - jnp.max(logits, axis=-1, keepdims=True))
    probs = ex / jnp.sum(ex, axis=-1, keepdims=True)
    lane = lax.broadcasted_iota(jnp.int32, probs.shape, 1)
    p1 = jnp.max(probs, axis=-1, keepdims=True)
    i1 = jnp.min(jnp.where(probs == p1, lane, LANES), axis=-1, keepdims=True)
    rest = jnp.where(lane == i1, -1.0, probs)
    p2 = jnp.max(rest, axis=-1, keepdims=True)
    i2 = jnp.min(jnp.where(rest == p2, lane, LANES), axis=-1, keepdims=True)
    den = p1 + p2
    tm = probs.shape[0]
    chosen = (lane == i1) | (lane == i2)
    before = _tri01(tm, lambda r, c: c < r)
    seen = _dot(before, chosen.astype(BF16)) + cnt_scr[...]
    rank1 = jnp.sum(jnp.where(lane == i1, seen, 0.0), axis=-1, keepdims=True)
    rank2 = jnp.sum(jnp.where(lane == i2, seen, 0.0), axis=-1, keepdims=True)
    cnt_scr[...] += jnp.sum(chosen.astype(F32), axis=0, keepdims=True)
    fields = {META_G1: p1 / den, META_G2: p2 / den, META_E1: i1.astype(F32), META_E2: i2.astype(F32),
              META_R1: rank1, META_R2: rank2}
    meta = jnp.zeros(probs.shape, F32)
    for k, v in fields.items():
        meta = jnp.where(lane == k, v, meta)
    return meta


def _router_params(w_router, b_router):
    d, ne = w_router.shape
    wr = jnp.zeros((d, LANES), BF16).at[:, :ne].set(w_router.astype(BF16))
    br = jnp.full((1, LANES), -1e30, F32).at[0, :ne].set(b_router)
    return wr, br


def _row_copy(src, src_row, dst, dst_row, sem):
    return pltpu.make_async_copy(src.at[pl.ds(src_row, 1)], dst.at[pl.ds(dst_row, 1)], sem)


def _dispatch_kernel(dst1_ref, dst2_ref, x_ref, zeros_hbm, xs_hbm, sem, *, tr):
    del zeros_hbm
    base = pl.program_id(0) * tr

    def each_row(fn):
        def body(r, _):
            t = base + r
            fn(_row_copy(x_ref, r, xs_hbm, dst1_ref[t], sem))
            fn(_row_copy(x_ref, r, xs_hbm, dst2_ref[t], sem))
            return 0

        lax.fori_loop(0, tr, body, 0, unroll=DMA_LOOP_UNROLL)

    each_row(lambda c: c.start())
    each_row(lambda c: c.wait())


def _dispatch(x, dst1, dst2, n_slots, tr):
    t, d = x.shape
    grid_spec = pltpu.PrefetchScalarGridSpec(
        num_scalar_prefetch=2,
        grid=(t // tr,),
        in_specs=[pl.BlockSpec((tr, d), lambda i, d1, d2: (i, 0)), pl.BlockSpec(memory_space=pl.ANY)],
        out_specs=pl.BlockSpec(memory_space=pl.ANY),
        scratch_shapes=[pltpu.SemaphoreType.DMA(())],
    )
    return pl.pallas_call(
        functools.partial(_dispatch_kernel, tr=tr),
        grid_spec=grid_spec,
        out_shape=jax.ShapeDtypeStruct((n_slots, d), F32),
        input_output_aliases={3: 0},
        compiler_params=_cparams("arbitrary"),
        name="moe_dispatch",
    )(dst1, dst2, x, jnp.zeros((n_slots, d), F32))


def _ffn_grouped_kernel(te_ref, tv_ref, x_ref, gpre_ref, wg_ref, wu_ref, wd_ref, y_ref, xn_scr):
    del te_ref
    i, f = pl.program_id(0), pl.program_id(1)
    valid = tv_ref[i] != 0

    @pl.when(f == 0)
    def _():
        xn_scr[...] = _rms(x_ref[...], gpre_ref[...]).astype(BF16)
        y_ref[...] = jnp.zeros_like(y_ref)

    @pl.when(valid)
    def _():
        xn = xn_scr[...]
        h = (jax.nn.silu(_dot(xn, wg_ref[...])) * _dot(xn, wu_ref[...])).astype(BF16)
        y_ref[...] += _dot(h, wd_ref[...])


def _ffn_grouped(xs, tile_expert, tile_valid, g_pre, w_gate, w_up, w_down, tm, tf):
    n_slots, d = xs.shape
    ff = w_gate.shape[2]
    row = pl.BlockSpec((tm, d), lambda i, f, te, tv: (i, 0))
    grid_spec = pltpu.PrefetchScalarGridSpec(
        num_scalar_prefetch=2,
        grid=(n_slots // tm, ff // tf),
        in_specs=[row,
                  pl.BlockSpec((1, d), lambda i, f, te, tv: (0, 0)),
                  pl.BlockSpec((None, d, tf), lambda i, f, te, tv: (te[i], 0, f)),
                  pl.BlockSpec((None, d, tf), lambda i, f, te, tv: (te[i], 0, f)),
                  pl.BlockSpec((None, tf, d), lambda i, f, te, tv: (te[i], f, 0))],
        out_specs=row,
        scratch_shapes=[pltpu.VMEM((tm, d), BF16)],
    )
    return pl.pallas_call(
        _ffn_grouped_kernel,
        grid_spec=grid_spec,
        out_shape=jax.ShapeDtypeStruct((n_slots, d), F32),
        compiler_params=_cparams("parallel", "arbitrary"),
        name="moe_ffn_grouped",
    )(tile_expert, tile_valid, xs, g_pre, w_gate, w_up, w_down)


def _combine_kernel(dst1_ref, dst2_ref, y_hbm, x_ref, meta_ref, gpost_ref, o_ref, ybuf, sems, *, tm, blk0):
    i, n = pl.program_id(0), pl.num_programs(0)

    def rows(step, slot, fn):
        base = (blk0 + step) * tm

        def body(r, _):
            t = base + r
            fn(_row_copy(y_hbm, dst1_ref[t], ybuf.at[slot, 0], r, sems.at[slot]))
            fn(_row_copy(y_hbm, dst2_ref[t], ybuf.at[slot, 1], r, sems.at[slot]))
            return 0

        lax.fori_loop(0, tm, body, 0, unroll=DMA_LOOP_UNROLL)

    @pl.when(i == 0)
    def _():
        rows(0, 0, lambda c: c.start())

    for slot in range(2):
        @pl.when((i + 1 < n) & ((i + 1) % 2 == slot))
        def _(slot=slot):
            rows(i + 1, slot, lambda c: c.start())

    for slot in range(2):
        @pl.when(i % 2 == slot)
        def _(slot=slot):
            rows(i, slot, lambda c: c.wait())
            meta = meta_ref[...]
            g1 = meta[:, META_G1:META_G1 + 1]
            g2 = meta[:, META_G2:META_G2 + 1]
            mix = g1 * ybuf[slot, 0] + g2 * ybuf[slot, 1]
            o_ref[...] = x_ref[...] + _rms(mix, gpost_ref[...])


def _combine(y, x, meta, dst1, dst2, g_post, tm, row0, n_rows):
    d = x.shape[1]
    blk0 = row0 // tm
    grid_spec = pltpu.PrefetchScalarGridSpec(
        num_scalar_prefetch=2,
        grid=(n_rows // tm,),
        in_specs=[pl.BlockSpec(memory_space=pl.ANY),
                  pl.BlockSpec((tm, d), lambda i, d1, d2: (blk0 + i, 0)),
                  pl.BlockSpec((tm, LANES), lambda i, d1, d2: (blk0 + i, 0)),
                  pl.BlockSpec((1, d), lambda i, d1, d2: (0, 0))],
        out_specs=pl.BlockSpec((tm, d), lambda i, d1, d2: (i, 0)),
        scratch_shapes=[pltpu.VMEM((2, 2, tm, d), F32), pltpu.SemaphoreType.DMA((2,))],
    )
    return pl.pallas_call(
        functools.partial(_combine_kernel, tm=tm, blk0=blk0),
        grid_spec=grid_spec,
        out_shape=jax.ShapeDtypeStruct((n_rows, d), F32),
        compiler_params=_cparams("arbitrary"),
        name="moe_combine",
    )(dst1, dst2, y, x, meta, g_post)


def _moe_plan(meta, counts, n_experts, tm):
    t = meta.shape[0]
    cnt = counts[0, :n_experts].astype(jnp.int32)
    padded = ((cnt + tm - 1) // tm) * tm
    ends = jnp.cumsum(padded)
    offs = ends - padded
    e1, e2 = meta[:, META_E1].astype(jnp.int32), meta[:, META_E2].astype(jnp.int32)
    dst1 = offs[e1] + meta[:, META_R1].astype(jnp.int32)
    dst2 = offs[e2] + meta[:, META_R2].astype(jnp.int32)
    n_tiles = (2 * t + n_experts * (tm - 1)) // tm
    starts = jnp.arange(n_tiles, dtype=jnp.int32) * tm
    tile_expert = jnp.minimum(jnp.sum(starts[:, None] >= ends[None, :], axis=1), n_experts - 1).astype(jnp.int32)
    tile_valid = (starts < ends[-1]).astype(jnp.int32)
    return dst1, dst2, tile_expert, tile_valid, n_tiles * tm


def _log_sigmoid(z):
    return jnp.minimum(z, 0.0) - jnp.log(1.0 + jnp.exp(-jnp.abs(z)))


def _fox_proj_prompt_kernel(x_ref, gpre_ref, wq_ref, wkvt_ref, wft_ref, bf_ref,
                            q_ref, kt_ref, vt_ref, lft_ref, *, scale):
    xn = _rms(x_ref[...], gpre_ref[...]).astype(BF16)
    q_ref[...] = (_dot(xn, wq_ref[...]) * scale).astype(BF16)
    d = kt_ref.shape[0]
    kvt = _dot_nt(wkvt_ref[...], xn)
    kt_ref[...] = kvt[:d]
    vt_ref[...] = kvt[d:]
    lft_ref[...] = _log_sigmoid(_dot_nt(wft_ref[...], xn) + bf_ref[...])


def _fox_proj_prompt(x, g_pre, wq, wkvt, wft, b_f, scale, tm):
    nb, nl, d = x.shape
    nh = wft.shape[0]
    const = lambda b, i: (0, 0)
    return pl.pallas_call(
        functools.partial(_fox_proj_prompt_kernel, scale=scale),
        grid=(nb, nl // tm),
        in_specs=[pl.BlockSpec((None, tm, d), lambda b, i: (b, i, 0)),
                  pl.BlockSpec((1, d), const),
                  pl.BlockSpec(wq.shape, const),
                  pl.BlockSpec(wkvt.shape, const),
                  pl.BlockSpec(wft.shape, const),
                  pl.BlockSpec((nh, 1), const)],
        out_specs=[pl.BlockSpec((None, tm, d), lambda b, i: (b, i, 0)),
                   pl.BlockSpec((None, d, tm), lambda b, i: (b, 0, i)),
                   pl.BlockSpec((None, d, tm), lambda b, i: (b, 0, i)),
                   pl.BlockSpec((None, nh, tm), lambda b, i: (b, 0, i))],
        out_shape=[jax.ShapeDtypeStruct((nb, nl, d), BF16),
                   jax.ShapeDtypeStruct((nb, d, nl), F32),
                   jax.ShapeDtypeStruct((nb, d, nl), F32),
                   jax.ShapeDtypeStruct((nb, nh, nl), F32)],
        compiler_params=_cparams("parallel", "parallel"),
        name="fox_proj_prompt",
    )(x, g_pre, wq, wkvt, wft, b_f)


def _tri01(n, rel):
    r = lax.broadcasted_iota(jnp.int32, (n, n), 0)
    c = lax.broadcasted_iota(jnp.int32, (n, n), 1)
    return rel(r, c).astype(BF16)


def _cumsum_lanes_kernel(x_ref, o_ref):
    upper = _tri01(LANES, lambda r, c: r <= c)
    carry = jnp.zeros((x_ref.shape[0], 1), F32)
    for blk in range(x_ref.shape[1] // LANES):
        sl = slice(blk * LANES, (blk + 1) * LANES)
        cum = _dot_exact_rhs01(x_ref[:, sl], upper) + carry
        o_ref[:, sl] = cum
        carry = cum[:, LANES - 1:LANES]


def _cumsum_lanes(x):
    nb, r, nl = x.shape
    spec = pl.BlockSpec((None, r, nl), lambda b: (b, 0, 0))
    return pl.pallas_call(
        _cumsum_lanes_kernel, grid=(nb,), in_specs=[spec], out_specs=spec,
        out_shape=jax.ShapeDtypeStruct(x.shape, F32),
        compiler_params=_cparams("parallel"), name="logf_cumsum",
    )(x)


def _prompt_attn_part(q_ref, kt_ref, vt_ref, c_ref, o_ref, hpair, qi, *, tq, hd):
    heads = LANES // hd
    half = tq // 2
    acc_w = hd + 2 * SUBLANES
    q0 = pl.multiple_of(qi * tq, tq)
    hslice = [slice(hh * hd, (hh + 1) * hd) for hh in range(heads)]
    qs = [q_ref[pl.ds(q0, tq), hs] for hs in hslice]

    def attend(hh, q_rows, m, acc, k0, width, causal):
        hs = hslice[hh]
        kt = kt_ref[hs, pl.ds(k0, width)].astype(BF16)
        vt = jnp.concatenate([vt_ref[hs, pl.ds(k0, width)].astype(BF16),
                              jnp.ones((acc_w - hd, width), BF16)], axis=0)
        cc = c_ref[pl.ds(hpair * heads + hh, 1), pl.ds(k0, width)]
        s = _dot(q_rows, kt) - cc
        if causal:
            row = lax.broadcasted_iota(jnp.int32, s.shape, 0)
            col = lax.broadcasted_iota(jnp.int32, s.shape, 1)
            s = jnp.where(row >= col, s, NEG_INF)
        blocks = [s[:, j * LANES:(j + 1) * LANES] for j in range(width // LANES)]
        m_new = jnp.maximum(m, jnp.max(functools.reduce(jnp.maximum, blocks), axis=-1, keepdims=True))
        alpha = jnp.exp(m - m_new)
        p = jnp.concatenate([jnp.exp(blk - m_new).astype(BF16) for blk in blocks], axis=1)
        return m_new, alpha[:, :acc_w] * acc + _dot_nt(p, vt)

    def full_tile(j, carry):
        k0 = pl.multiple_of(j * tq, tq)
        return tuple(attend(hh, qs[hh], *carry[hh], k0, tq, False) for hh in range(heads))

    init = tuple((jnp.full((tq, LANES), NEG_INF, F32), jnp.zeros((tq, acc_w), F32)) for _ in hslice)
    carry = lax.fori_loop(0, qi // 2, lambda jj, cr: full_tile(2 * jj + 1, full_tile(2 * jj, cr)), init)
    carry = lax.cond(qi % 2 == 1, lambda cr: full_tile(qi - 1, cr), lambda cr: cr, carry)
    for hh, hs in enumerate(hslice):
        m, acc = attend(hh, qs[hh], *carry[hh], q0, half, True)
        _, acc_lo = attend(hh, qs[hh][half:], m[half:], acc[half:], pl.multiple_of(q0 + half, half), half, True)
        acc = jnp.concatenate([acc[:half], acc_lo], axis=0)
        o_ref[:, hs] = (acc[:, :hd] / acc[:, hd:hd + 1]).astype(o_ref.dtype)


def _fox_proj_sample_kernel(x_ref, gpre_ref, w_ref, wf_ref, bf_ref, q_ref, k_ref, v_ref, lf_ref, *, scale):
    xn = _rms(x_ref[...], gpre_ref[...]).astype(BF16)
    d = q_ref.shape[1]
    proj = _dot(xn, w_ref[...])
    q_ref[...] = proj[:, :d] * scale
    k_ref[...] = proj[:, d:2 * d]
    v_ref[...] = proj[:, 2 * d:]
    lf_ref[...] = _log_sigmoid(_dot(xn, wf_ref[...]) + bf_ref[...])


def _fox_proj_sample(x, g_pre, w_qkv, w_f, b_f, scale):
    t, d = x.shape
    nh = w_f.shape[1]
    wf = jnp.zeros((d, LANES), BF16).at[:, :nh].set(w_f)
    bf = jnp.zeros((1, LANES), F32).at[0, :nh].set(b_f)
    full = lambda a: pl.BlockSpec(a.shape, lambda i: (0,) * a.ndim)
    row = pl.BlockSpec((t, d), lambda i: (0, 0))
    return pl.pallas_call(
        functools.partial(_fox_proj_sample_kernel, scale=scale),
        grid=(1,),
        in_specs=[row, full(g_pre), full(w_qkv), full(wf), full(bf)],
        out_specs=[row, row, row, pl.BlockSpec((t, LANES), lambda i: (0, 0))],
        out_shape=[jax.ShapeDtypeStruct((t, d), F32)] * 3 + [jax.ShapeDtypeStruct((t, LANES), F32)],
        compiler_params=_cparams("arbitrary"),
        name="fox_proj_sample",
    )(x, g_pre, w_qkv, wf, bf)


def _paged_attn_part(q_ref, kc_refs, vc_refs, lfc_refs, kn_ref, vn_ref, lfn_ref, o_ref,
                     qbd, m_scr, l_scr, acc_scr, carry, bias_scr, p, n_chunks, *, ns, hd):
    gp = len(kc_refs)
    n_groups, g_rows, g_width = qbd.shape
    rows, d = n_groups * g_rows, n_groups * g_width
    nh = d // hd
    gmask = (lax.broadcasted_iota(jnp.int32, (g_rows, g_width), 0) // ns
             == lax.broadcasted_iota(jnp.int32, (g_rows, g_width), 1) // hd)
    expand = (lax.broadcasted_iota(jnp.int32, (rows, nh), 0) // ns
              == lax.broadcasted_iota(jnp.int32, (rows, nh), 1)).astype(BF16)
    g_cols = lambda g: slice(g * g_width, (g + 1) * g_width)
    g_rws = lambda g: slice(g * g_rows, (g + 1) * g_rows)

    @pl.when(p == 0)
    def _():
        for g in range(n_groups):
            qt = jnp.concatenate([q_ref[:, g_cols(g)]] * (g_rows // ns), axis=0)
            qbd[g] = jnp.where(gmask, qt, 0.0).astype(BF16)
        m_scr[...] = jnp.full_like(m_scr, NEG_INF)
        l_scr[...] = jnp.zeros_like(l_scr)
        acc_scr[...] = jnp.zeros_like(acc_scr)
        carry[...] = jnp.zeros_like(carry)

    def per_group(fn):
        return jnp.concatenate([fn(g) for g in range(n_groups)], axis=0)

    def update(s, pv_of_group):
        m_new = jnp.maximum(m_scr[...], jnp.max(s, axis=-1, keepdims=True))
        alpha = jnp.exp(m_scr[...] - m_new)
        pr = jnp.exp(s - m_new)
        l_scr[...] = alpha * l_scr[...] + jnp.sum(pr, axis=-1, keepdims=True)
        pr = pr.astype(BF16)
        acc_scr[...] = alpha * acc_scr[...] + per_group(lambda g: pv_of_group(g, pr[g_rws(g)]))
        m_scr[...] = m_new

    lfs = [r[...] for r in lfc_refs]
    after = _tri01(LANES, lambda r, c: r > c)
    within = _dot_exact_rhs01(jnp.concatenate(lfs, axis=0), after)
    run = carry[...]
    for k in reversed(range(gp)):
        w_k = within[k * nh:(k + 1) * nh]
        bias_scr[:, k * LANES:(k + 1) * LANES] = run + w_k
        run = run + (w_k[:, 0:1] + lfs[k][:, 0:1])
    carry[...] = run
    def pages(refs, g):
        return jnp.concatenate([r[g_cols(g), :].astype(BF16) for r in refs], axis=1)

    s = per_group(lambda g: _dot(qbd[g], pages(kc_refs, g)))
    s = jnp.concatenate([s[h * ns:(h + 1) * ns] + bias_scr[h:h + 1, :] for h in range(nh)], axis=0)
    update(s, lambda g, pr: _dot_nt(pr, pages(vc_refs, g)))

    @pl.when(p == n_chunks - 1)
    def _():
        upto = _tri01(LANES, lambda r, c: r <= c)
        cn = _dot_exact_rhs01(lfn_ref[...], upto)
        pad = jnp.zeros((LANES - ns, d), F32)
        kn = jnp.concatenate([kn_ref[...], pad], axis=0).astype(BF16)
        vn = jnp.concatenate([vn_ref[...], pad], axis=0).astype(BF16)
        sn = per_group(lambda g: _dot_nt(qbd[g], kn[:, g_cols(g)])) - _dot_exact_lhs01(expand, cn)
        qpos = lax.broadcasted_iota(jnp.int32, (rows, LANES), 0) % ns
        kpos = lax.broadcasted_iota(jnp.int32, (rows, LANES), 1)
        update(jnp.where(kpos <= qpos, sn, NEG_INF), lambda g, pr: _dot(pr, vn[:, g_cols(g)]))
        o = acc_scr[...] / l_scr[...]
        for g in range(n_groups):
            og = jnp.where(gmask, o[g_rws(g)], 0.0)
            o_ref[:, g_cols(g)] = functools.reduce(
                jnp.add, [og[h * ns:(h + 1) * ns] for h in range(g_rows // ns)])


def _fox_attn_kernel(tbl_ref, *refs, gp, n_chunks, n_sample_steps, n_hpairs, n_qtiles, n_prompt_steps,
                     ns, hd, tq):
    del tbl_ref
    sq_ref, refs = refs[0], refs[1:]
    kc_refs, vc_refs, lfc_refs = refs[:gp], refs[gp:2 * gp], refs[2 * gp:3 * gp]
    kn_ref, vn_ref, lfn_ref, q_ref, kt_ref, vt_ref, c_ref, os_ref, op_ref = refs[3 * gp:3 * gp + 9]
    decode_scratch = refs[3 * gp + 9:]
    s = pl.program_id(0)

    @pl.when(s < n_sample_steps)
    def _():
        _paged_attn_part(sq_ref, kc_refs, vc_refs, lfc_refs, kn_ref, vn_ref, lfn_ref, os_ref, *decode_scratch,
                         s % n_chunks, n_chunks, ns=ns, hd=hd)

    @pl.when(s < n_prompt_steps)
    def _():
        _prompt_attn_part(q_ref, kt_ref, vt_ref, c_ref, op_ref, (s // n_qtiles) % n_hpairs, s % n_qtiles,
                          tq=tq, hd=hd)


def _fox_attn(page_table, sq, kc, vc, lfc, kn, vn, lfn, q, kt, vt, c, hd):
    db, ns, d = sq.shape
    n_pages = page_table.shape[1]
    nh = d // hd
    rows = nh * ns
    page = kc.shape[2]
    assert rows == LANES and page == LANES
    gp = next(g for g in (16, 8, 4, 2, 1) if n_pages % g == 0)
    n_chunks = n_pages // gp
    n_sample_steps = db * n_chunks

    nb, nl, _ = q.shape
    tq = min(nl, 512)
    assert nl % tq == 0 and tq % (2 * LANES) == 0
    n_hpairs, n_qtiles = d // LANES, nl // tq
    n_prompt_steps = nb * n_hpairs * n_qtiles

    def sample_pos(s):
        s = jnp.minimum(s, n_sample_steps - 1)
        return s // n_chunks, s % n_chunks

    def prompt_pos(s):
        s = jnp.minimum(s, n_prompt_steps - 1)
        return s // (n_hpairs * n_qtiles), (s // n_qtiles) % n_hpairs, s % n_qtiles

    seq = pl.BlockSpec((None, ns, d), lambda s, tbl: (sample_pos(s)[0], 0, 0))

    def pg(k):
        def index(s, tbl):
            b, p = sample_pos(s)
            return tbl[b, (n_chunks - 1 - p) * gp + k], 0, 0
        return index

    def pidx(fn):
        return lambda s, tbl: fn(*prompt_pos(s))

    grid_spec = pltpu.PrefetchScalarGridSpec(
        num_scalar_prefetch=1,
        grid=(max(n_sample_steps, n_prompt_steps),),
        in_specs=([seq]
                  + [pl.BlockSpec((None, d, page), pg(k)) for k in range(gp)]
                  + [pl.BlockSpec((None, d, page), pg(k)) for k in range(gp)]
                  + [pl.BlockSpec((None, nh, page), pg(k)) for k in range(gp)]
                  + [seq, seq, pl.BlockSpec((None, nh, LANES), lambda s, tbl: (sample_pos(s)[0], 0, 0))]
                  + [pl.BlockSpec((None, nl, LANES), pidx(lambda b, h, i: (b, 0, h))),
                     pl.BlockSpec((None, LANES, nl), pidx(lambda b, h, i: (b, h, 0))),
                     pl.BlockSpec((None, LANES, nl), pidx(lambda b, h, i: (b, h, 0))),
                     pl.BlockSpec((None, nh, nl), pidx(lambda b, h, i: (b, 0, 0)))]),
        out_specs=[seq, pl.BlockSpec((None, tq, LANES), pidx(lambda b, h, i: (b, i, h)))],
        scratch_shapes=[pltpu.VMEM((d // V7X_MXU_DIM, rows * V7X_MXU_DIM // d, V7X_MXU_DIM), BF16),
                        pltpu.VMEM((rows, 1), F32),
                        pltpu.VMEM((rows, 1), F32),
                        pltpu.VMEM((rows, V7X_MXU_DIM), F32),
                        pltpu.VMEM((nh, 1), F32),
                        pltpu.VMEM((nh, gp * page), F32)],
    )
    return pl.pallas_call(
        functools.partial(_fox_attn_kernel, gp=gp, n_chunks=n_chunks, n_sample_steps=n_sample_steps,
                          n_hpairs=n_hpairs, n_qtiles=n_qtiles, n_prompt_steps=n_prompt_steps,
                          ns=ns, hd=hd, tq=tq),
        grid_spec=grid_spec,
        out_shape=[jax.ShapeDtypeStruct((db, ns, d), F32), jax.ShapeDtypeStruct((nb, nl, d), BF16)],
        compiler_params=_cparams("arbitrary"),
        name="fox_attn",
    )(page_table, sq, *([kc] * gp), *([vc] * gp), *([lfc] * gp), kn, vn, lfn, q, kt, vt, c)


def _div_tile(t, candidates):
    return next((tm for tm in candidates if t % tm == 0), t)


def _row_tile(t):
    return _div_tile(t, (1024, 512, 256, 128))


def _ff_tile(ff):
    return _div_tile(ff, (7 * V7X_MXU_DIM, 4 * V7X_MXU_DIM, 2 * V7X_MXU_DIM, V7X_MXU_DIM, LANES))


def kernel(x_prompt, x_sample, state_ssm_re, state_ssm_im, cache_k, cache_v, cache_logf, page_table, norm_mix_pre, norm_mix_post, norm_ffn_pre, norm_ffn_post, ssm_w_in, ssm_lambda_re, ssm_lambda_im, ssm_log_dt, ssm_b_re, ssm_b_im, ssm_c_re, ssm_c_im, ssm_d, ssm_w_glu, ssm_b_glu, ssm_w_out, fox_w_in, fox_b_f, fox_w_out, ffn_w_gate, ffn_w_up, ffn_w_down, moe_w_router, moe_b_router, moe_w_gate, moe_w_up, moe_w_down):
    nb, nl, d = x_prompt.shape
    db, ns, _ = x_sample.shape
    n_groups, n_state = ssm_lambda_re.shape[1:]
    nh = fox_b_f.shape[1]
    hd = d // nh
    scale = hd ** -0.5
    ng = d // LANES
    streams = [x_prompt.reshape(nb * nl, d), x_sample.reshape(db * ns, d)]
    dims = [(nb, nl), (db, ns)]
    tms = [_row_tile(x.shape[0]) for x in streams]
    vec = lambda a: a.reshape(1, -1)

    li = 0
    w_in = ssm_w_in[li].astype(BF16)
    w_glu = ssm_w_glu[li].astype(BF16)
    w_out = ssm_w_out[li].astype(BF16)
    s5w = _s5_prepare(ssm_lambda_re[li], ssm_lambda_im[li], ssm_log_dt[li], ssm_b_re[li], ssm_b_im[li],
                      ssm_c_re[li], ssm_c_im[li])
    wg, wu, wd = (w[li].astype(BF16) for w in (ffn_w_gate, ffn_w_up, ffn_w_down))
    h0s = [jnp.zeros((ng, nb, 2 * (n_groups // ng) * n_state), F32),
           _s5_state_in(state_ssm_re[li], state_ssm_im[li], ng)]
    ssm_states = []
    for si in range(2):
        x, (b_, l_), tm = streams[si], dims[si], tms[si]
        u = _norm_matmul(x, vec(norm_mix_pre[0]), w_in, tm)
        g_act, h_fin = _s5_core(u.reshape(b_, l_, d), h0s[si], *s5w, vec(ssm_d[li]))
        ssm_states.append(_s5_state_out(h_fin, n_groups, n_state))
        x = _glu_out(g_act.reshape(b_ * l_, d), x, w_glu, vec(ssm_b_glu[li]), w_out, vec(norm_mix_post[0]), tm)
        streams[si] = _ffn(x, vec(norm_ffn_pre[0]), wg, wu, wd, vec(norm_ffn_post[0]), min(tm, FFN_ROW_TILE),
                           _ff_tile(wg.shape[1]))

    w_fox = fox_w_in[li]
    w_fox_t = w_fox.T
    wq = w_fox[:, :d].astype(BF16)
    wkvt = w_fox_t[d:3 * d].astype(BF16)
    wft = w_fox_t[3 * d:].astype(BF16)
    w_o = fox_w_out[li].astype(BF16)

    xp = streams[0]
    q, kt, vt, lft = _fox_proj_prompt(xp.reshape(nb, nl, d), vec(norm_mix_pre[1]), wq, wkvt, wft,
                                      fox_b_f[li].reshape(nh, 1), scale, min(nl, 1024))
    c = _cumsum_lanes(lft)
    tp, tsm = nb * nl, db * ns
    k_prompt = kt.reshape(nb, nh, hd, nl).transpose(0, 3, 1, 2)[None]
    v_prompt = vt.reshape(nb, nh, hd, nl).transpose(0, 3, 1, 2)[None]
    logf_prompt = lft.transpose(0, 2, 1)[None]

    xs = streams[1]
    qs, ks, vs, lfs = _fox_proj_sample(xs, vec(norm_mix_pre[1]), w_fox[:, :3 * d].astype(BF16),
                                       w_fox[:, 3 * d:].astype(BF16), fox_b_f[li], scale)
    n_pool, page = cache_k.shape[1], cache_k.shape[2]
    kc = cache_k[li].transpose(0, 2, 3, 1).reshape(n_pool, d, page)
    vc = cache_v[li].transpose(0, 2, 3, 1).reshape(n_pool, d, page)
    lfc = cache_logf[li].transpose(0, 2, 1)
    lfs = lfs[:, :nh]
    lfn = jnp.zeros((db, nh, LANES), F32).at[:, :, :ns].set(lfs.reshape(db, ns, nh).transpose(0, 2, 1))
    os_, o = _fox_attn(page_table, qs.reshape(db, ns, d), kc, vc, lfc,
                       ks.reshape(db, ns, d), vs.reshape(db, ns, d), lfn, q, kt, vt, c, hd)
    small = (256, 128, 64, 32, 16, 8)
    x_all, meta, counts = _matmul_post_route(
        o.reshape(tp, d), xp, os_.reshape(tsm, d), xs, w_o, vec(norm_mix_post[1]), vec(norm_ffn_pre[1]),
        *_router_params(moe_w_router[li], moe_b_router[li]), _div_tile(tsm, small))
    k_sample = ks.reshape(1, db, ns, nh, hd)
    v_sample = vs.reshape(1, db, ns, nh, hd)
    logf_sample = lfs.reshape(1, db, ns, nh)

    mg, mu, md = (w[li].astype(BF16) for w in (moe_w_gate, moe_w_up, moe_w_down))
    n_experts = mg.shape[0]
    dst1, dst2, tile_expert, tile_valid, n_slots = _moe_plan(meta, counts, n_experts, FFN_ROW_TILE)
    x_sorted = _dispatch(x_all, dst1, dst2, n_slots, _div_tile(tp + tsm, small))
    y_sorted = _ffn_grouped(x_sorted, tile_expert, tile_valid, vec(norm_ffn_pre[1]), mg, mu, md, FFN_ROW_TILE,
                            _ff_tile(mg.shape[2]))
    outs = [_combine(y_sorted, x_all, meta, dst1, dst2, vec(norm_ffn_post[1]), _div_tile(n, small), r0, n)
            for r0, n in ((0, tp), (tp, tsm))]

    (re_p, im_p), (re_s, im_s) = ssm_states
    return (outs[0].reshape(nb, nl, d), outs[1].reshape(db, ns, d),
            re_p[None], im_p[None], re_s[None], im_s[None],
            k_prompt, v_prompt, logf_prompt, k_sample, v_sample, logf_sample)
```

```python
import functools

import jax
import jax.numpy as jnp
from jax import lax
from jax.experimental import pallas as pl
from jax.experimental.pallas import tpu as pltpu

F32 = jnp.float32
BF16 = jnp.bfloat16
RMS_EPS = 1e-6
LANES = 128
SUBLANES = 8
V7X_MXU_DIM = 256
V7X_VMEM_LIMIT_BYTES = 56 << 20
S5_CHUNK = 8
S5_SCAN_UNROLL = 8
FFN_ROW_TILE = 512
DMA_LOOP_UNROLL = 8
NEG_INF = float("-inf")


def _cparams(*sem):
    return pltpu.CompilerParams(dimension_semantics=sem, vmem_limit_bytes=V7X_VMEM_LIMIT_BYTES)


def _rms(x, g):
    return x * lax.rsqrt(jnp.mean(x * x, axis=-1, keepdims=True) + RMS_EPS) * g


def _dot(a, b):
    return jnp.dot(a, b, preferred_element_type=F32)


def _dot_nt(a, b):
    return lax.dot_general(a, b, (((1,), (1,)), ((), ())), preferred_element_type=F32)


def _split3(x):
    hi = x.astype(BF16)
    r = x - hi.astype(F32)
    mid = r.astype(BF16)
    lo = (r - mid.astype(F32)).astype(BF16)
    return hi, mid, lo


def _dot_exact_rhs01(x, m01):
    hi, mid, lo = _split3(x)
    return _dot(hi, m01) + _dot(mid, m01) + _dot(lo, m01)


def _dot_exact_lhs01(m01, x):
    hi, mid, lo = _split3(x)
    return _dot(m01, hi) + _dot(m01, mid) + _dot(m01, lo)


def _norm_matmul_kernel(x_ref, g_ref, w_ref, o_ref):
    xn = _rms(x_ref[...], g_ref[...]).astype(BF16)
    o_ref[...] = _dot(xn, w_ref[...])


def _norm_matmul(x, g, w, tm):
    t, d = x.shape
    n = w.shape[1]
    return pl.pallas_call(
        _norm_matmul_kernel,
        grid=(t // tm,),
        in_specs=[pl.BlockSpec((tm, d), lambda i: (i, 0)),
                  pl.BlockSpec((1, d), lambda i: (0, 0)),
                  pl.BlockSpec((d, n), lambda i: (0, 0))],
        out_specs=pl.BlockSpec((tm, n), lambda i: (i, 0)),
        out_shape=jax.ShapeDtypeStruct((t, n), F32),
        compiler_params=_cparams("parallel"),
        name="norm_matmul",
    )(x, g, w)


def _s5_prepare(lam_re, lam_im, log_dt, b_re, b_im, c_re, c_im):
    tc = S5_CHUNK
    g, p = lam_re.shape
    c = b_re.shape[-1]
    ng = (g * c) // LANES
    gl = g // ng
    hp = lax.Precision.HIGHEST
    dt = jnp.exp(log_dt)[:, None]
    mag = jnp.exp(lam_re * dt)
    ar = mag * jnp.cos(lam_im * dt)
    ai = mag * jnp.sin(lam_im * dt)
    den = lam_re * lam_re + lam_im * lam_im
    nr = ar - 1.0
    coef_r = (nr * lam_re + ai * lam_im) / den
    coef_i = (ai * lam_re - nr * lam_im) / den
    bbr = coef_r[..., None] * b_re - coef_i[..., None] * b_im
    bbi = coef_r[..., None] * b_im + coef_i[..., None] * b_re
    prs, pis = [jnp.ones_like(ar)], [jnp.zeros_like(ar)]
    for _ in range(tc):
        nr_, ni_ = prs[-1] * ar - pis[-1] * ai, prs[-1] * ai + pis[-1] * ar
        prs.append(nr_)
        pis.append(ni_)
    pr = jnp.stack(prs)
    pi = jnp.stack(pis)
    bbr_t, bbi_t = bbr.transpose(0, 2, 1), bbi.transpose(0, 2, 1)
    prs_ = pr[tc - 1 - jnp.arange(tc)][:, :, None, :]
    pis_ = pi[tc - 1 - jnp.arange(tc)][:, :, None, :]
    ms = jnp.concatenate([prs_ * bbr_t - pis_ * bbi_t, prs_ * bbi_t + pis_ * bbr_t], axis=-1)
    m_state = ms.reshape(tc, ng, gl, c, 2 * p).transpose(1, 0, 2, 3, 4).reshape(ng, tc * gl * c, 2 * p)

    er = pr[:tc, :, :, None] * bbr - pi[:tc, :, :, None] * bbi
    ei = pr[:tc, :, :, None] * bbi + pi[:tc, :, :, None] * bbr
    kk = (jnp.einsum("gdp,kgpc->gckd", c_re, er, precision=hp)
          - jnp.einsum("gdp,kgpc->gckd", c_im, ei, precision=hp))
    kt = jnp.stack([jnp.pad(kk[:, :, :tc - t], ((0, 0), (0, 0), (t, 0), (0, 0))).reshape(g, c, tc * c)
                    for t in range(tc)], axis=1)
    m_intra = kt.reshape(ng, gl, tc, c, tc * c).transpose(0, 2, 1, 3, 4).reshape(ng, tc * gl * c, tc * c)

    pr1 = pr[1:].transpose(1, 2, 0)[:, :, :, None]
    pi1 = pi[1:].transpose(1, 2, 0)[:, :, :, None]
    cre = c_re.transpose(0, 2, 1)[:, :, None, :]
    cim = c_im.transpose(0, 2, 1)[:, :, None, :]
    hh = jnp.stack([cre * pr1 - cim * pi1, -(cre * pi1 + cim * pr1)])
    m_carry = hh.reshape(2, ng, gl * p, tc * c).transpose(1, 0, 2, 3).reshape(ng, 2 * gl * p, tc * c)

    w_state = _s5_expand(m_state, gl, row_group=c, col_block=gl * p, col_sub=p)
    w_intra = _s5_expand(m_intra, gl, row_group=c, col_block=gl * c, col_sub=c)
    w_carry = _s5_expand(m_carry, gl, row_group=p, col_block=gl * c, col_sub=c)
    a_pow = jnp.concatenate([pr[tc].reshape(ng, 1, gl * p), pi[tc].reshape(ng, 1, gl * p)], axis=-1)
    return w_state, w_intra, w_carry, a_pow


def _s5_expand_kernel(m_ref, o_ref, *, gl, row_group, col_block, col_sub):
    rows, cols = o_ref.shape
    s_i = lax.broadcasted_iota(jnp.int32, (m_ref.shape[1], cols), 0)
    j_i = lax.broadcasted_iota(jnp.int32, (m_ref.shape[1], cols), 1)
    place = (s_i == (j_i // col_block) * col_sub + j_i % col_sub).astype(BF16)
    spread = _dot(m_ref[...].astype(BF16), place)
    r_g = (lax.broadcasted_iota(jnp.int32, (rows, cols), 0) // row_group) % gl
    c_g = (lax.broadcasted_iota(jnp.int32, (rows, cols), 1) % col_block) // col_sub
    o_ref[...] = jnp.where(r_g == c_g, spread, 0.0).astype(o_ref.dtype)


def _s5_expand(m, gl, row_group, col_block, col_sub):
    ng, rows, k = m.shape
    cols = k * gl
    return pl.pallas_call(
        functools.partial(_s5_expand_kernel, gl=gl, row_group=row_group, col_block=col_block, col_sub=col_sub),
        grid=(ng,),
        in_specs=[pl.BlockSpec((None, rows, k), lambda a: (a, 0, 0))],
        out_specs=pl.BlockSpec((None, rows, cols), lambda a: (a, 0, 0)),
        out_shape=jax.ShapeDtypeStruct((ng, rows, cols), BF16),
        compiler_params=_cparams("parallel"),
        name="s5_expand",
    )(m)


def _s5_core_kernel(u_ref, h0_ref, ws_ref, wi_ref, wh_ref, ap_ref, d_ref, g_ref, hout_ref,
                    s_scr, hs_scr, hcar, *, nb, nj):
    tc = S5_CHUNK
    rows = nb * nj
    nk = s_scr.shape[0]
    half = nk // 2
    lane_blk = lambda k: slice(k * LANES, (k + 1) * LANES)

    @pl.when(pl.program_id(1) == 0)
    def _():
        hcar[...] = h0_ref[...]

    def u_at(t):
        if nj == 1:
            return u_ref[:, t, :]
        return u_ref[:, pl.ds(t, nj, stride=tc), :].reshape(rows, LANES)

    us = [u_at(t) for t in range(tc)]
    x = jnp.concatenate([v.astype(BF16) for v in us], axis=1)
    s_all = _dot(x, ws_ref[...])
    for k in range(nk):
        s_scr[k] = s_all[:, lane_blk(k)]

    a_pow = [ap_ref[:, lane_blk(k)] for k in range(nk)]

    def rows_at(j):
        return slice(None) if nj == 1 else pl.ds(j, nb, stride=nj)

    def step(j, h):
        for k in range(nk):
            hs_scr[k, rows_at(j), :] = h[k]
        new_re, new_im = [], []
        for k in range(half):
            hr, hi, ar, ai = h[k], h[k + half], a_pow[k], a_pow[k + half]
            new_re.append(ar * hr - ai * hi + s_scr[k, rows_at(j), :])
            new_im.append(ar * hi + ai * hr + s_scr[k + half, rows_at(j), :])
        return tuple(new_re + new_im)

    h = tuple(hcar[:, lane_blk(k)] for k in range(nk))
    h = step(0, h) if nj == 1 else lax.fori_loop(0, nj, step, h, unroll=S5_SCAN_UNROLL)
    for k in range(nk):
        hcar[:, lane_blk(k)] = h[k]
        hout_ref[:, lane_blk(k)] = h[k]

    hs = jnp.concatenate([hs_scr[k].astype(BF16) for k in range(nk)], axis=1)
    y = _dot(x, wi_ref[...]) + _dot(hs, wh_ref[...])
    d = d_ref[...]
    for t in range(tc):
        gt = jax.nn.gelu(y[:, lane_blk(t)] + d * us[t])
        if nj == 1:
            g_ref[:, t, :] = gt
        else:
            g_ref[:, pl.ds(t, nj, stride=tc), :] = gt.reshape(nb, nj, LANES)


def _s5_core(u, h0, w_state, w_intra, w_carry, a_pow, d_skip):
    nb, nl, d = u.shape
    ng = d // LANES
    ts = min(nl, 512)
    nj = ts // S5_CHUNK
    sw = w_state.shape[2]
    kern = functools.partial(_s5_core_kernel, nb=nb, nj=nj)
    return pl.pallas_call(
        kern,
        grid=(ng, nl // ts),
        in_specs=[pl.BlockSpec((nb, ts, LANES), lambda g, s: (0, s, g)),
                  pl.BlockSpec((None, nb, sw), lambda g, s: (g, 0, 0)),
                  pl.BlockSpec((None,) + w_state.shape[1:], lambda g, s: (g, 0, 0)),
                  pl.BlockSpec((None,) + w_intra.shape[1:], lambda g, s: (g, 0, 0)),
                  pl.BlockSpec((None,) + w_carry.shape[1:], lambda g, s: (g, 0, 0)),
                  pl.BlockSpec((None, 1, sw), lambda g, s: (g, 0, 0)),
                  pl.BlockSpec((1, LANES), lambda g, s: (0, g))],
        out_specs=[pl.BlockSpec((nb, ts, LANES), lambda g, s: (0, s, g)),
                   pl.BlockSpec((None, nb, sw), lambda g, s: (g, 0, 0))],
        out_shape=[jax.ShapeDtypeStruct((nb, nl, d), F32),
                   jax.ShapeDtypeStruct((ng, nb, sw), F32)],
        scratch_shapes=[pltpu.VMEM((sw // LANES, nb * nj, LANES), F32),
                        pltpu.VMEM((sw // LANES, nb * nj, LANES), F32),
                        pltpu.VMEM((nb, sw), F32)],
        compiler_params=_cparams("parallel", "arbitrary"),
        name="s5_core",
    )(u, h0, w_state, w_intra, w_carry, a_pow, d_skip)


def _s5_state_in(re, im, ng):
    nb = re.shape[0]
    h = jnp.concatenate([re.reshape(nb, ng, -1), im.reshape(nb, ng, -1)], axis=-1)
    return h.transpose(1, 0, 2)


def _s5_state_out(h, g, p):
    ng, nb, sw = h.shape
    h = h.transpose(1, 0, 2)
    return h[..., :sw // 2].reshape(nb, g, p), h[..., sw // 2:].reshape(nb, g, p)


def _glu_out_kernel(g_ref, x_ref, wglu_ref, bglu_ref, wout_ref, gpost_ref, o_ref):
    g = g_ref[...]
    t = _dot(g.astype(BF16), wglu_ref[...]) + bglu_ref[...]
    z = (g * jax.nn.sigmoid(t)).astype(BF16)
    m = _dot(z, wout_ref[...])
    o_ref[...] = x_ref[...] + _rms(m, gpost_ref[...])


def _glu_out(g, x, w_glu, b_glu, w_out, g_post, tm):
    t, d = x.shape
    row = pl.BlockSpec((tm, d), lambda i: (i, 0))
    vec = pl.BlockSpec((1, d), lambda i: (0, 0))
    mat = pl.BlockSpec((d, d), lambda i: (0, 0))
    return pl.pallas_call(
        _glu_out_kernel,
        grid=(t // tm,),
        in_specs=[row, row, mat, vec, mat, vec],
        out_specs=row,
        out_shape=jax.ShapeDtypeStruct((t, d), F32),
        compiler_params=_cparams("parallel"),
        name="glu_out",
    )(g, x, w_glu, b_glu, w_out, g_post)


def _matmul_post_route_kernel(a1_ref, x1_ref, a2_ref, x2_ref, w_ref, gpost_ref, gpre_ref, wr_ref, br_ref,
                              o_ref, meta_ref, cnt_ref, cnt_scr, *, n1):
    i = pl.program_id(0)

    @pl.when(i == 0)
    def _():
        cnt_scr[...] = jnp.zeros_like(cnt_scr)

    def emit(a_ref, x_ref):
        m = _dot(a_ref[...].astype(BF16), w_ref[...])
        x_new = x_ref[...] + _rms(m, gpost_ref[...])
        o_ref[...] = x_new
        meta_ref[...] = _route(x_new, gpre_ref, wr_ref, br_ref, cnt_scr)
        cnt_ref[...] = cnt_scr[...]

    pl.when(i < n1)(lambda: emit(a1_ref, x1_ref))
    pl.when(i >= n1)(lambda: emit(a2_ref, x2_ref))


def _matmul_post_route(a1, x1, a2, x2, w, g_post, g_pre_next, wr, br, tm):
    (t1, d), t2 = x1.shape, x2.shape[0]
    n1, n2 = t1 // tm, t2 // tm
    first = pl.BlockSpec((tm, d), lambda i: (jnp.minimum(i, n1 - 1), 0))
    second = pl.BlockSpec((tm, d), lambda i: (jnp.maximum(i - n1, 0), 0))
    vec = pl.BlockSpec((1, d), lambda i: (0, 0))
    lane_vec = pl.BlockSpec((1, LANES), lambda i: (0, 0))
    return pl.pallas_call(
        functools.partial(_matmul_post_route_kernel, n1=n1),
        grid=(n1 + n2,),
        in_specs=[first, first, second, second, pl.BlockSpec((d, d), lambda i: (0, 0)), vec,
                  vec, pl.BlockSpec((d, LANES), lambda i: (0, 0)), lane_vec],
        out_specs=[pl.BlockSpec((tm, d), lambda i: (i, 0)), pl.BlockSpec((tm, LANES), lambda i: (i, 0)), lane_vec],
        out_shape=[jax.ShapeDtypeStruct((t1 + t2, d), F32), jax.ShapeDtypeStruct((t1 + t2, LANES), F32),
                   jax.ShapeDtypeStruct((1, LANES), F32)],
        scratch_shapes=[pltpu.VMEM((1, LANES), F32)],
        compiler_params=_cparams("arbitrary"),
        name="matmul_post_route",
    )(a1, x1, a2, x2, w, g_post, g_pre_next, wr, br)


def _ffn_kernel(x_ref, gpre_ref, wg_ref, wu_ref, wd_ref, gpost_ref, o_ref, xn_scr, acc_scr):
    f, nf = pl.program_id(1), pl.num_programs(1)

    @pl.when(f == 0)
    def _():
        xn_scr[...] = _rms(x_ref[...], gpre_ref[...]).astype(BF16)
        acc_scr[...] = jnp.zeros_like(acc_scr)

    xn = xn_scr[...]
    h = (jax.nn.silu(_dot(xn, wg_ref[...])) * _dot(xn, wu_ref[...])).astype(BF16)
    acc_scr[...] += _dot(h, wd_ref[...])

    @pl.when(f == nf - 1)
    def _():
        o_ref[...] = x_ref[...] + _rms(acc_scr[...], gpost_ref[...])


def _ffn(x, g_pre, w_gate, w_up, w_down, g_post, tm, tf):
    t, d = x.shape
    ff = w_gate.shape[1]
    row = pl.BlockSpec((tm, d), lambda i, f: (i, 0))
    vec = pl.BlockSpec((1, d), lambda i, f: (0, 0))
    return pl.pallas_call(
        _ffn_kernel,
        grid=(t // tm, ff // tf),
        in_specs=[row, vec,
                  pl.BlockSpec((d, tf), lambda i, f: (0, f)),
                  pl.BlockSpec((d, tf), lambda i, f: (0, f)),
                  pl.BlockSpec((tf, d), lambda i, f: (f, 0)),
                  vec],
        out_specs=row,
        out_shape=jax.ShapeDtypeStruct((t, d), F32),
        scratch_shapes=[pltpu.VMEM((tm, d), BF16), pltpu.VMEM((tm, d), F32)],
        compiler_params=_cparams("parallel", "arbitrary"),
        name="ffn_dense",
    )(x, g_pre, w_gate, w_up, w_down, g_post)


META_G1, META_G2, META_E1, META_E2, META_R1, META_R2 = range(6)


def _route(x, gpre_ref, wr_ref, br_ref, cnt_scr):
    xn = _rms(x, gpre_ref[...]).astype(BF16)
    logits = _dot(xn, wr_ref[...]) + br_ref[...]
    ex = jnp.exp(logits - jnp.max(logits, axis=-1, keepdims=True))
    probs = ex / jnp.sum(ex, axis=-1, keepdims=True)
    lane = lax.broadcasted_iota(jnp.int32, probs.shape, 1)
    p1 = jnp.max(probs, axis=-1, keepdims=True)
    i1 = jnp.min(jnp.where(probs == p1, lane, LANES), axis=-1, keepdims=True)
    rest = jnp.where(lane == i1, -1.0, probs)
    p2 = jnp.max(rest, axis=-1, keepdims=True)
    i2 = jnp.min(jnp.where(rest == p2, lane, LANES), axis=-1, keepdims=True)
    den = p1 + p2
    tm = probs.shape[0]
    chosen = (lane == i1) | (lane == i2)
    before = _tri01(tm, lambda r, c: c < r)
    seen = _dot(before, chosen.astype(BF16)) + cnt_scr[...]
    rank1 = jnp.sum(jnp.where(lane == i1, seen, 0.0), axis=-1, keepdims=True)
    rank2 = jnp.sum(jnp.where(lane == i2, seen, 0.0), axis=-1, keepdims=True)
    cnt_scr[...] += jnp.sum(chosen.astype(F32), axis=0, keepdims=True)
    fields = {META_G1: p1 / den, META_G2: p2 / den, META_E1: i1.astype(F32), META_E2: i2.astype(F32),
              META_R1: rank1, META_R2: rank2}
    meta = jnp.zeros(probs.shape, F32)
    for k, v in fields.items():
        meta = jnp.where(lane == k, v, meta)
    return meta


def _router_params(w_router, b_router):
    d, ne = w_router.shape
    wr = jnp.zeros((d, LANES), BF16).at[:, :ne].set(w_router.astype(BF16))
    br = jnp.full((1, LANES), -1e30, F32).at[0, :ne].set(b_router)
    return wr, br


def _row_copy(src, src_row, dst, dst_row, sem):
    return pltpu.make_async_copy(src.at[pl.ds(src_row, 1)], dst.at[pl.ds(dst_row, 1)], sem)


def _dispatch_kernel(dst1_ref, dst2_ref, x_ref, zeros_hbm, xs_hbm, sem, *, tr):
    del zeros_hbm
    base = pl.program_id(0) * tr

    def each_row(fn):
        def body(r, _):
            t = base + r
            fn(_row_copy(x_ref, r, xs_hbm, dst1_ref[t], sem), 0)
            fn(_row_copy(x_ref, r, xs_hbm, dst2_ref[t], sem), 1)
            return 0

        lax.fori_loop(0, tr, body, 0, unroll=DMA_LOOP_UNROLL)

    each_row(lambda c, dma_thread: c.start(priority=dma_thread))
    each_row(lambda c, dma_thread: c.wait())


def _dispatch(x, dst1, dst2, n_slots, tr):
    t, d = x.shape
    grid_spec = pltpu.PrefetchScalarGridSpec(
        num_scalar_prefetch=2,
        grid=(t // tr,),
        in_specs=[pl.BlockSpec((tr, d), lambda i, d1, d2: (i, 0)), pl.BlockSpec(memory_space=pl.ANY)],
        out_specs=pl.BlockSpec(memory_space=pl.ANY),
        scratch_shapes=[pltpu.SemaphoreType.DMA(())],
    )
    return pl.pallas_call(
        functools.partial(_dispatch_kernel, tr=tr),
        grid_spec=grid_spec,
        out_shape=jax.ShapeDtypeStruct((n_slots, d), F32),
        input_output_aliases={3: 0},
        compiler_params=_cparams("arbitrary"),
        name="moe_dispatch",
    )(dst1, dst2, x, jnp.zeros((n_slots, d), F32))


def _ffn_grouped_kernel(te_ref, tv_ref, x_ref, gpre_ref, wg_ref, wu_ref, wd_ref, y_ref, xn_scr):
    del te_ref
    i, f = pl.program_id(0), pl.program_id(1)
    valid = tv_ref[i] != 0

    @pl.when(f == 0)
    def _():
        xn_scr[...] = _rms(x_ref[...], gpre_ref[...]).astype(BF16)
        y_ref[...] = jnp.zeros_like(y_ref)

    @pl.when(valid)
    def _():
        xn = xn_scr[...]
        h = (jax.nn.silu(_dot(xn, wg_ref[...])) * _dot(xn, wu_ref[...])).astype(BF16)
        y_ref[...] += _dot(h, wd_ref[...])


def _ffn_grouped(xs, tile_expert, tile_valid, g_pre, w_gate, w_up, w_down, tm, tf):
    n_slots, d = xs.shape
    ff = w_gate.shape[2]
    row = pl.BlockSpec((tm, d), lambda i, f, te, tv: (i, 0))
    grid_spec = pltpu.PrefetchScalarGridSpec(
        num_scalar_prefetch=2,
        grid=(n_slots // tm, ff // tf),
        in_specs=[row,
                  pl.BlockSpec((1, d), lambda i, f, te, tv: (0, 0)),
                  pl.BlockSpec((None, d, tf), lambda i, f, te, tv: (te[i], 0, f)),
                  pl.BlockSpec((None, d, tf), lambda i, f, te, tv: (te[i], 0, f)),
                  pl.BlockSpec((None, tf, d), lambda i, f, te, tv: (te[i], f, 0))],
        out_specs=row,
        scratch_shapes=[pltpu.VMEM((tm, d), BF16)],
    )
    return pl.pallas_call(
        _ffn_grouped_kernel,
        grid_spec=grid_spec,
        out_shape=jax.ShapeDtypeStruct((n_slots, d), F32),
        compiler_params=_cparams("parallel", "arbitrary"),
        name="moe_ffn_grouped",
    )(tile_expert, tile_valid, xs, g_pre, w_gate, w_up, w_down)


def _combine_kernel(dst1_ref, dst2_ref, y_hbm, x_ref, meta_ref, gpost_ref, o_ref, ybuf, sems, *, tm, blk0):
    i, n = pl.program_id(0), pl.num_programs(0)

    def rows(step, slot, fn):
        base = (blk0 + step) * tm

        def body(r, _):
            t = base + r
            fn(_row_copy(y_hbm, dst1_ref[t], ybuf.at[slot, 0], r, sems.at[slot]), 0)
            fn(_row_copy(y_hbm, dst2_ref[t], ybuf.at[slot, 1], r, sems.at[slot]), 1)
            return 0

        lax.fori_loop(0, tm, body, 0, unroll=DMA_LOOP_UNROLL)

    @pl.when(i == 0)
    def _():
        rows(0, 0, lambda c, dma_thread: c.start(priority=dma_thread))

    for slot in range(2):
        @pl.when((i + 1 < n) & ((i + 1) % 2 == slot))
        def _(slot=slot):
            rows(i + 1, slot, lambda c, dma_thread: c.start(priority=dma_thread))

    for slot in range(2):
        @pl.when(i % 2 == slot)
        def _(slot=slot):
            rows(i, slot, lambda c, dma_thread: c.wait())
            meta = meta_ref[...]
            g1 = meta[:, META_G1:META_G1 + 1]
            g2 = meta[:, META_G2:META_G2 + 1]
            mix = g1 * ybuf[slot, 0] + g2 * ybuf[slot, 1]
            o_ref[...] = x_ref[...] + _rms(mix, gpost_ref[...])


def _combine(y, x, meta, dst1, dst2, g_post, tm, row0, n_rows):
    d = x.shape[1]
    blk0 = row0 // tm
    grid_spec = pltpu.PrefetchScalarGridSpec(
        num_scalar_prefetch=2,
        grid=(n_rows // tm,),
        in_specs=[pl.BlockSpec(memory_space=pl.ANY),
                  pl.BlockSpec((tm, d), lambda i, d1, d2: (blk0 + i, 0)),
                  pl.BlockSpec((tm, LANES), lambda i, d1, d2: (blk0 + i, 0)),
                  pl.BlockSpec((1, d), lambda i, d1, d2: (0, 0))],
        out_specs=pl.BlockSpec((tm, d), lambda i, d1, d2: (i, 0)),
        scratch_shapes=[pltpu.VMEM((2, 2, tm, d), F32), pltpu.SemaphoreType.DMA((2,))],
    )
    return pl.pallas_call(
        functools.partial(_combine_kernel, tm=tm, blk0=blk0),
        grid_spec=grid_spec,
        out_shape=jax.ShapeDtypeStruct((n_rows, d), F32),
        compiler_params=_cparams("arbitrary"),
        name="moe_combine",
    )(dst1, dst2, y, x, meta, g_post)


def _moe_plan(meta, counts, n_experts, tm):
    t = meta.shape[0]
    cnt = counts[0, :n_experts].astype(jnp.int32)
    padded = ((cnt + tm - 1) // tm) * tm
    ends = jnp.cumsum(padded)
    offs = ends - padded
    e1, e2 = meta[:, META_E1].astype(jnp.int32), meta[:, META_E2].astype(jnp.int32)
    dst1 = offs[e1] + meta[:, META_R1].astype(jnp.int32)
    dst2 = offs[e2] + meta[:, META_R2].astype(jnp.int32)
    n_tiles = (2 * t + n_experts * (tm - 1)) // tm
    starts = jnp.arange(n_tiles, dtype=jnp.int32) * tm
    tile_expert = jnp.minimum(jnp.sum(starts[:, None] >= ends[None, :], axis=1), n_experts - 1).astype(jnp.int32)
    tile_valid = (starts < ends[-1]).astype(jnp.int32)
    return dst1, dst2, tile_expert, tile_valid, n_tiles * tm


def _log_sigmoid(z):
    return jnp.minimum(z, 0.0) - jnp.log(1.0 + jnp.exp(-jnp.abs(z)))


def _fox_proj_prompt_kernel(x_ref, gpre_ref, wq_ref, wkvt_ref, wft_ref, bf_ref,
                            q_ref, kt_ref, vt_ref, lft_ref, *, scale):
    xn = _rms(x_ref[...], gpre_ref[...]).astype(BF16)
    q_ref[...] = (_dot(xn, wq_ref[...]) * scale).astype(BF16)
    d = kt_ref.shape[0]
    kvt = _dot_nt(wkvt_ref[...], xn)
    kt_ref[...] = kvt[:d]
    vt_ref[...] = kvt[d:]
    lft_ref[...] = _log_sigmoid(_dot_nt(wft_ref[...], xn) + bf_ref[...])


def _fox_proj_prompt(x, g_pre, wq, wkvt, wft, b_f, scale, tm):
    nb, nl, d = x.shape
    nh = wft.shape[0]
    const = lambda b, i: (0, 0)
    return pl.pallas_call(
        functools.partial(_fox_proj_prompt_kernel, scale=scale),
        grid=(nb, nl // tm),
        in_specs=[pl.BlockSpec((None, tm, d), lambda b, i: (b, i, 0)),
                  pl.BlockSpec((1, d), const),
                  pl.BlockSpec(wq.shape, const),
                  pl.BlockSpec(wkvt.shape, const),
                  pl.BlockSpec(wft.shape, const),
                  pl.BlockSpec((nh, 1), const)],
        out_specs=[pl.BlockSpec((None, tm, d), lambda b, i: (b, i, 0)),
                   pl.BlockSpec((None, d, tm), lambda b, i: (b, 0, i)),
                   pl.BlockSpec((None, d, tm), lambda b, i: (b, 0, i)),
                   pl.BlockSpec((None, nh, tm), lambda b, i: (b, 0, i))],
        out_shape=[jax.ShapeDtypeStruct((nb, nl, d), BF16),
                   jax.ShapeDtypeStruct((nb, d, nl), F32),
                   jax.ShapeDtypeStruct((nb, d, nl), F32),
                   jax.ShapeDtypeStruct((nb, nh, nl), F32)],
        compiler_params=_cparams("parallel", "parallel"),
        name="fox_proj_prompt",
    )(x, g_pre, wq, wkvt, wft, b_f)


def _tri01(n, rel):
    r = lax.broadcasted_iota(jnp.int32, (n, n), 0)
    c = lax.broadcasted_iota(jnp.int32, (n, n), 1)
    return rel(r, c).astype(BF16)


def _cumsum_lanes_kernel(x_ref, o_ref):
    upper = _tri01(LANES, lambda r, c: r <= c)
    carry = jnp.zeros((x_ref.shape[0], 1), F32)
    for blk in range(x_ref.shape[1] // LANES):
        sl = slice(blk * LANES, (blk + 1) * LANES)
        cum = _dot_exact_rhs01(x_ref[:, sl], upper) + carry
        o_ref[:, sl] = cum
        carry = cum[:, LANES - 1:LANES]


def _cumsum_lanes(x):
    nb, r, nl = x.shape
    spec = pl.BlockSpec((None, r, nl), lambda b: (b, 0, 0))
    return pl.pallas_call(
        _cumsum_lanes_kernel, grid=(nb,), in_specs=[spec], out_specs=spec,
        out_shape=jax.ShapeDtypeStruct(x.shape, F32),
        compiler_params=_cparams("parallel"), name="logf_cumsum",
    )(x)


def _prompt_attn_part(q_ref, kt_ref, vt_ref, c_ref, o_ref, hpair, qi, *, tq, hd):
    heads = LANES // hd
    half = tq // 2
    acc_w = hd + 2 * SUBLANES
    q0 = pl.multiple_of(qi * tq, tq)
    hslice = [slice(hh * hd, (hh + 1) * hd) for hh in range(heads)]
    qs = [q_ref[pl.ds(q0, tq), hs] for hs in hslice]

    def attend(hh, q_rows, m, acc, k0, width, causal):
        hs = hslice[hh]
        kt = kt_ref[hs, pl.ds(k0, width)].astype(BF16)
        vt = jnp.concatenate([vt_ref[hs, pl.ds(k0, width)].astype(BF16),
                              jnp.ones((acc_w - hd, width), BF16)], axis=0)
        cc = c_ref[pl.ds(hpair * heads + hh, 1), pl.ds(k0, width)]
        s = _dot(q_rows, kt) - cc
        if causal:
            row = lax.broadcasted_iota(jnp.int32, s.shape, 0)
            col = lax.broadcasted_iota(jnp.int32, s.shape, 1)
            s = jnp.where(row >= col, s, NEG_INF)
        blocks = [s[:, j * LANES:(j + 1) * LANES] for j in range(width // LANES)]
        m_new = jnp.maximum(m, jnp.max(functools.reduce(jnp.maximum, blocks), axis=-1, keepdims=True))
        alpha = jnp.exp(m - m_new)
        p = jnp.concatenate([jnp.exp(blk - m_new).astype(BF16) for blk in blocks], axis=1)
        return m_new, alpha[:, :acc_w] * acc + _dot_nt(p, vt)

    def full_tile(j, carry):
        k0 = pl.multiple_of(j * tq, tq)
        return tuple(attend(hh, qs[hh], *carry[hh], k0, tq, False) for hh in range(heads))

    init = tuple((jnp.full((tq, LANES), NEG_INF, F32), jnp.zeros((tq, acc_w), F32)) for _ in hslice)
    carry = lax.fori_loop(0, qi // 2, lambda jj, cr: full_tile(2 * jj + 1, full_tile(2 * jj, cr)), init)
    carry = lax.cond(qi % 2 == 1, lambda cr: full_tile(qi - 1, cr), lambda cr: cr, carry)
    for hh, hs in enumerate(hslice):
        m, acc = attend(hh, qs[hh], *carry[hh], q0, half, True)
        _, acc_lo = attend(hh, qs[hh][half:], m[half:], acc[half:], pl.multiple_of(q0 + half, half), half, True)
        acc = jnp.concatenate([acc[:half], acc_lo], axis=0)
        o_ref[:, hs] = (acc[:, :hd] / acc[:, hd:hd + 1]).astype(o_ref.dtype)


def _fox_proj_sample_kernel(x_ref, gpre_ref, w_ref, wf_ref, bf_ref, q_ref, k_ref, v_ref, lf_ref, *, scale):
    xn = _rms(x_ref[...], gpre_ref[...]).astype(BF16)
    d = q_ref.shape[1]
    proj = _dot(xn, w_ref[...])
    q_ref[...] = proj[:, :d] * scale
    k_ref[...] = proj[:, d:2 * d]
    v_ref[...] = proj[:, 2 * d:]
    lf_ref[...] = _log_sigmoid(_dot(xn, wf_ref[...]) + bf_ref[...])


def _fox_proj_sample(x, g_pre, w_qkv, w_f, b_f, scale):
    t, d = x.shape
    nh = w_f.shape[1]
    wf = jnp.zeros((d, LANES), BF16).at[:, :nh].set(w_f)
    bf = jnp.zeros((1, LANES), F32).at[0, :nh].set(b_f)
    full = lambda a: pl.BlockSpec(a.shape, lambda i: (0,) * a.ndim)
    row = pl.BlockSpec((t, d), lambda i: (0, 0))
    return pl.pallas_call(
        functools.partial(_fox_proj_sample_kernel, scale=scale),
        grid=(1,),
        in_specs=[row, full(g_pre), full(w_qkv), full(wf), full(bf)],
        out_specs=[row, row, row, pl.BlockSpec((t, LANES), lambda i: (0, 0))],
        out_shape=[jax.ShapeDtypeStruct((t, d), F32)] * 3 + [jax.ShapeDtypeStruct((t, LANES), F32)],
        compiler_params=_cparams("arbitrary"),
        name="fox_proj_sample",
    )(x, g_pre, w_qkv, wf, bf)


def _paged_attn_part(q_ref, kc_refs, vc_refs, lfc_refs, kn_ref, vn_ref, lfn_ref, o_ref,
                     qbd, m_scr, l_scr, acc_scr, carry, bias_scr, p, n_chunks, *, ns, hd):
    gp = len(kc_refs)
    n_groups, g_rows, g_width = qbd.shape
    rows, d = n_groups * g_rows, n_groups * g_width
    nh = d // hd
    gmask = (lax.broadcasted_iota(jnp.int32, (g_rows, g_width), 0) // ns
             == lax.broadcasted_iota(jnp.int32, (g_rows, g_width), 1) // hd)
    expand = (lax.broadcasted_iota(jnp.int32, (rows, nh), 0) // ns
              == lax.broadcasted_iota(jnp.int32, (rows, nh), 1)).astype(BF16)
    g_cols = lambda g: slice(g * g_width, (g + 1) * g_width)
    g_rws = lambda g: slice(g * g_rows, (g + 1) * g_rows)

    @pl.when(p == 0)
    def _():
        for g in range(n_groups):
            qt = jnp.concatenate([q_ref[:, g_cols(g)]] * (g_rows // ns), axis=0)
            qbd[g] = jnp.where(gmask, qt, 0.0).astype(BF16)
        m_scr[...] = jnp.full_like(m_scr, NEG_INF)
        l_scr[...] = jnp.zeros_like(l_scr)
        acc_scr[...] = jnp.zeros_like(acc_scr)
        carry[...] = jnp.zeros_like(carry)

    def per_group(fn):
        return jnp.concatenate([fn(g) for g in range(n_groups)], axis=0)

    def update(s, pv_of_group):
        m_new = jnp.maximum(m_scr[...], jnp.max(s, axis=-1, keepdims=True))
        alpha = jnp.exp(m_scr[...] - m_new)
        pr = jnp.exp(s - m_new)
        l_scr[...] = alpha * l_scr[...] + jnp.sum(pr, axis=-1, keepdims=True)
        pr = pr.astype(BF16)
        acc_scr[...] = alpha * acc_scr[...] + per_group(lambda g: pv_of_group(g, pr[g_rws(g)]))
        m_scr[...] = m_new

    lfs = [r[...] for r in lfc_refs]
    after = _tri01(LANES, lambda r, c: r > c)
    within = _dot_exact_rhs01(jnp.concatenate(lfs, axis=0), after)
    run = carry[...]
    for k in reversed(range(gp)):
        w_k = within[k * nh:(k + 1) * nh]
        bias_scr[:, k * LANES:(k + 1) * LANES] = run + w_k
        run = run + (w_k[:, 0:1] + lfs[k][:, 0:1])
    carry[...] = run
    def pages(refs, g):
        return jnp.concatenate([r[g_cols(g), :].astype(BF16) for r in refs], axis=1)

    s = per_group(lambda g: _dot(qbd[g], pages(kc_refs, g)))
    s = jnp.concatenate([s[h * ns:(h + 1) * ns] + bias_scr[h:h + 1, :] for h in range(nh)], axis=0)
    update(s, lambda g, pr: _dot_nt(pr, pages(vc_refs, g)))

    @pl.when(p == n_chunks - 1)
    def _():
        upto = _tri01(LANES, lambda r, c: r <= c)
        cn = _dot_exact_rhs01(lfn_ref[...], upto)
        pad = jnp.zeros((LANES - ns, d), F32)
        kn = jnp.concatenate([kn_ref[...], pad], axis=0).astype(BF16)
        vn = jnp.concatenate([vn_ref[...], pad], axis=0).astype(BF16)
        sn = per_group(lambda g: _dot_nt(qbd[g], kn[:, g_cols(g)])) - _dot_exact_lhs01(expand, cn)
        qpos = lax.broadcasted_iota(jnp.int32, (rows, LANES), 0) % ns
        kpos = lax.broadcasted_iota(jnp.int32, (rows, LANES), 1)
        update(jnp.where(kpos <= qpos, sn, NEG_INF), lambda g, pr: _dot(pr, vn[:, g_cols(g)]))
        o = acc_scr[...] / l_scr[...]
        for g in range(n_groups):
            og = jnp.where(gmask, o[g_rws(g)], 0.0)
            o_ref[:, g_cols(g)] = functools.reduce(
                jnp.add, [og[h * ns:(h + 1) * ns] for h in range(g_rows // ns)])


def _fox_attn_kernel(tbl_ref, *refs, gp, n_chunks, n_sample_steps, n_hpairs, n_qtiles, n_prompt_steps,
                     ns, hd, tq):
    del tbl_ref
    sq_ref, refs = refs[0], refs[1:]
    kc_refs, vc_refs, lfc_refs = refs[:gp], refs[gp:2 * gp], refs[2 * gp:3 * gp]
    kn_ref, vn_ref, lfn_ref, q_ref, kt_ref, vt_ref, c_ref, os_ref, op_ref = refs[3 * gp:3 * gp + 9]
    decode_scratch = refs[3 * gp + 9:]
    s = pl.program_id(0)

    @pl.when(s < n_sample_steps)
    def _():
        _paged_attn_part(sq_ref, kc_refs, vc_refs, lfc_refs, kn_ref, vn_ref, lfn_ref, os_ref, *decode_scratch,
                         s % n_chunks, n_chunks, ns=ns, hd=hd)

    @pl.when(s < n_prompt_steps)
    def _():
        _prompt_attn_part(q_ref, kt_ref, vt_ref, c_ref, op_ref, (s // n_qtiles) % n_hpairs, s % n_qtiles,
                          tq=tq, hd=hd)


def _fox_attn(page_table, sq, kc, vc, lfc, kn, vn, lfn, q, kt, vt, c, hd):
    db, ns, d = sq.shape
    n_pages = page_table.shape[1]
    nh = d // hd
    rows = nh * ns
    page = kc.shape[2]
    assert rows == LANES and page == LANES
    gp = next(g for g in (16, 8, 4, 2, 1) if n_pages % g == 0)
    n_chunks = n_pages // gp
    n_sample_steps = db * n_chunks

    nb, nl, _ = q.shape
    tq = min(nl, 512)
    assert nl % tq == 0 and tq % (2 * LANES) == 0
    n_hpairs, n_qtiles = d // LANES, nl // tq
    n_prompt_steps = nb * n_hpairs * n_qtiles

    def sample_pos(s):
        s = jnp.minimum(s, n_sample_steps - 1)
        return s // n_chunks, s % n_chunks

    def prompt_pos(s):
        s = jnp.minimum(s, n_prompt_steps - 1)
        return s // (n_hpairs * n_qtiles), (s // n_qtiles) % n_hpairs, s % n_qtiles

    seq = pl.BlockSpec((None, ns, d), lambda s, tbl: (sample_pos(s)[0], 0, 0))

    def pg(k):
        def index(s, tbl):
            b, p = sample_pos(s)
            return tbl[b, (n_chunks - 1 - p) * gp + k], 0, 0
        return index

    def pidx(fn):
        return lambda s, tbl: fn(*prompt_pos(s))

    grid_spec = pltpu.PrefetchScalarGridSpec(
        num_scalar_prefetch=1,
        grid=(max(n_sample_steps, n_prompt_steps),),
        in_specs=([seq]
                  + [pl.BlockSpec((None, d, page), pg(k)) for k in range(gp)]
                  + [pl.BlockSpec((None, d, page), pg(k)) for k in range(gp)]
                  + [pl.BlockSpec((None, nh, page), pg(k)) for k in range(gp)]
                  + [seq, seq, pl.BlockSpec((None, nh, LANES), lambda s, tbl: (sample_pos(s)[0], 0, 0))]
                  + [pl.BlockSpec((None, nl, LANES), pidx(lambda b, h, i: (b, 0, h))),
                     pl.BlockSpec((None, LANES, nl), pidx(lambda b, h, i: (b, h, 0))),
                     pl.BlockSpec((None, LANES, nl), pidx(lambda b, h, i: (b, h, 0))),
                     pl.BlockSpec((None, nh, nl), pidx(lambda b, h, i: (b, 0, 0)))]),
        out_specs=[seq, pl.BlockSpec((None, tq, LANES), pidx(lambda b, h, i: (b, i, h)))],
        scratch_shapes=[pltpu.VMEM((d // V7X_MXU_DIM, rows * V7X_MXU_DIM // d, V7X_MXU_DIM), BF16),
                        pltpu.VMEM((rows, 1), F32),
                        pltpu.VMEM((rows, 1), F32),
                        pltpu.VMEM((rows, V7X_MXU_DIM), F32),
                        pltpu.VMEM((nh, 1), F32),
                        pltpu.VMEM((nh, gp * page), F32)],
    )
    return pl.pallas_call(
        functools.partial(_fox_attn_kernel, gp=gp, n_chunks=n_chunks, n_sample_steps=n_sample_steps,
                          n_hpairs=n_hpairs, n_qtiles=n_qtiles, n_prompt_steps=n_prompt_steps,
                          ns=ns, hd=hd, tq=tq),
        grid_spec=grid_spec,
        out_shape=[jax.ShapeDtypeStruct((db, ns, d), F32), jax.ShapeDtypeStruct((nb, nl, d), BF16)],
        compiler_params=_cparams("arbitrary"),
        name="fox_attn",
    )(page_table, sq, *([kc] * gp), *([vc] * gp), *([lfc] * gp), kn, vn, lfn, q, kt, vt, c)


def _div_tile(t, candidates):
    return next((tm for tm in candidates if t % tm == 0), t)


def _row_tile(t):
    return _div_tile(t, (1024, 512, 256, 128))


def _ff_tile(ff):
    return _div_tile(ff, (7 * V7X_MXU_DIM, 4 * V7X_MXU_DIM, 2 * V7X_MXU_DIM, V7X_MXU_DIM, LANES))


def kernel(x_prompt, x_sample, state_ssm_re, state_ssm_im, cache_k, cache_v, cache_logf, page_table, norm_mix_pre, norm_mix_post, norm_ffn_pre, norm_ffn_post, ssm_w_in, ssm_lambda_re, ssm_lambda_im, ssm_log_dt, ssm_b_re, ssm_b_im, ssm_c_re, ssm_c_im, ssm_d, ssm_w_glu, ssm_b_glu, ssm_w_out, fox_w_in, fox_b_f, fox_w_out, ffn_w_gate, ffn_w_up, ffn_w_down, moe_w_router, moe_b_router, moe_w_gate, moe_w_up, moe_w_down):
    nb, nl, d = x_prompt.shape
    db, ns, _ = x_sample.shape
    n_groups, n_state = ssm_lambda_re.shape[1:]
    nh = fox_b_f.shape[1]
    hd = d // nh
    scale = hd ** -0.5
    ng = d // LANES
    streams = [x_prompt.reshape(nb * nl, d), x_sample.reshape(db * ns, d)]
    dims = [(nb, nl), (db, ns)]
    tms = [_row_tile(x.shape[0]) for x in streams]
    vec = lambda a: a.reshape(1, -1)

    li = 0
    w_in = ssm_w_in[li].astype(BF16)
    w_glu = ssm_w_glu[li].astype(BF16)
    w_out = ssm_w_out[li].astype(BF16)
    s5w = _s5_prepare(ssm_lambda_re[li], ssm_lambda_im[li], ssm_log_dt[li], ssm_b_re[li], ssm_b_im[li],
                      ssm_c_re[li], ssm_c_im[li])
    wg, wu, wd = (w[li].astype(BF16) for w in (ffn_w_gate, ffn_w_up, ffn_w_down))
    h0s = [jnp.zeros((ng, nb, 2 * (n_groups // ng) * n_state), F32),
           _s5_state_in(state_ssm_re[li], state_ssm_im[li], ng)]
    ssm_states = []
    for si in range(2):
        x, (b_, l_), tm = streams[si], dims[si], tms[si]
        u = _norm_matmul(x, vec(norm_mix_pre[0]), w_in, tm)
        g_act, h_fin = _s5_core(u.reshape(b_, l_, d), h0s[si], *s5w, vec(ssm_d[li]))
        ssm_states.append(_s5_state_out(h_fin, n_groups, n_state))
        x = _glu_out(g_act.reshape(b_ * l_, d), x, w_glu, vec(ssm_b_glu[li]), w_out, vec(norm_mix_post[0]), tm)
        streams[si] = _ffn(x, vec(norm_ffn_pre[0]), wg, wu, wd, vec(norm_ffn_post[0]), min(tm, FFN_ROW_TILE),
                           _ff_tile(wg.shape[1]))

    w_fox = fox_w_in[li]
    w_fox_t = w_fox.T
    wq = w_fox[:, :d].astype(BF16)
    wkvt = w_fox_t[d:3 * d].astype(BF16)
    wft = w_fox_t[3 * d:].astype(BF16)
    w_o = fox_w_out[li].astype(BF16)

    xp = streams[0]
    q, kt, vt, lft = _fox_proj_prompt(xp.reshape(nb, nl, d), vec(norm_mix_pre[1]), wq, wkvt, wft,
                                      fox_b_f[li].reshape(nh, 1), scale, min(nl, 1024))
    c = _cumsum_lanes(lft)
    tp, tsm = nb * nl, db * ns
    k_prompt = kt.reshape(nb, nh, hd, nl).transpose(0, 3, 1, 2)[None]
    v_prompt = vt.reshape(nb, nh, hd, nl).transpose(0, 3, 1, 2)[None]
    logf_prompt = lft.transpose(0, 2, 1)[None]

    xs = streams[1]
    qs, ks, vs, lfs = _fox_proj_sample(xs, vec(norm_mix_pre[1]), w_fox[:, :3 * d].astype(BF16),
                                       w_fox[:, 3 * d:].astype(BF16), fox_b_f[li], scale)
    n_pool, page = cache_k.shape[1], cache_k.shape[2]
    kc = cache_k[li].transpose(0, 2, 3, 1).reshape(n_pool, d, page)
    vc = cache_v[li].transpose(0, 2, 3, 1).reshape(n_pool, d, page)
    lfc = cache_logf[li].transpose(0, 2, 1)
    lfs = lfs[:, :nh]
    lfn = jnp.zeros((db, nh, LANES), F32).at[:, :, :ns].set(lfs.reshape(db, ns, nh).transpose(0, 2, 1))
    os_, o = _fox_attn(page_table, qs.reshape(db, ns, d), kc, vc, lfc,
                       ks.reshape(db, ns, d), vs.reshape(db, ns, d), lfn, q, kt, vt, c, hd)
    small = (256, 128, 64, 32, 16, 8)
    x_all, meta, counts = _matmul_post_route(
        o.reshape(tp, d), xp, os_.reshape(tsm, d), xs, w_o, vec(norm_mix_post[1]), vec(norm_ffn_pre[1]),
        *_router_params(moe_w_router[li], moe_b_router[li]), _div_tile(tsm, small))
    k_sample = ks.reshape(1, db, ns, nh, hd)
    v_sample = vs.reshape(1, db, ns, nh, hd)
    logf_sample = lfs.reshape(1, db, ns, nh)

    mg, mu, md = (w[li].astype(BF16) for w in (moe_w_gate, moe_w_up, moe_w_down))
    n_experts = mg.shape[0]
    dst1, dst2, tile_expert, tile_valid, n_slots = _moe_plan(meta, counts, n_experts, FFN_ROW_TILE)
    x_sorted = _dispatch(x_all, dst1, dst2, n_slots, _div_tile(tp + tsm, small))
    y_sorted = _ffn_grouped(x_sorted, tile_expert, tile_valid, vec(norm_ffn_pre[1]), mg, mu, md, FFN_ROW_TILE,
                            _ff_tile(mg.shape[2]))
    outs = [_combine(y_sorted, x_all, meta, dst1, dst2, vec(norm_ffn_post[1]), _div_tile(n, small), r0, n)
            for r0, n in ((0, tp), (tp, tsm))]

    (re_p, im_p), (re_s, im_s) = ssm_states
    return (outs[0].reshape(nb, nl, d), outs[1].reshape(db, ns, d),
            re_p[None], im_p[None], re_s[None], im_s[None],
            k_prompt, v_prompt, logf_prompt, k_sample, v_sample, logf_sample)
```
